```python
import math
import jax, jax.numpy as jnp
from jax import lax
import numpy as np

D_MODEL = 2048
BATCH = 4
SEQ = 8192
DEPTH = 4

CHUNK = 64
Q_BLOCK = 128
EPS = 1e-6
ROPE_THETA = 10000.0

A_HEAD = 64
A_DIM = 3 * D_MODEL // 8
A_HEADS = A_DIM // A_HEAD
DECAY_LORA = max(32, int(round(1.8 * D_MODEL ** 0.5 / 32)) * 32)
AAA_LORA = max(32, int(round(1.8 * D_MODEL ** 0.5 / 32)) * 32)
GATE_LORA = max(32, int(round(0.6 * D_MODEL ** 0.8 / 32)) * 32)
RW_IN = 3 * A_DIM + DECAY_LORA + AAA_LORA + GATE_LORA
GN_EPS = 64e-5

B_QK = 64
B_V = 2 * B_QK
B_DIM = 3 * D_MODEL // 8
B_HEADS = B_DIM // B_V
B_IN = 2 * (B_HEADS * 2 * B_QK) + B_DIM

C_GROUP = 16
C_DIM = D_MODEL // 4
C_GROUPS = C_DIM // C_GROUP
C_STATE = 64

N_BRANCH = 3
N_IN = RW_IN + B_IN + C_DIM + N_BRANCH * D_MODEL

D_FF = ((8 * D_MODEL // 3) + 255) // 256 * 256
CONV_W = 3

kernel_name = 'hybrid_rwkv7_diffattn_s5_convglu_trunk'


def _rms_norm(x, g):
    xf = x.astype(jnp.float32)
    xf = xf * lax.rsqrt(jnp.mean(xf * xf, axis=-1, keepdims=True) + EPS)
    return (xf * g.astype(jnp.float32)).astype(x.dtype)


def _rope_tables(positions):
    inv_freq = ROPE_THETA ** (-jnp.arange(0, B_QK, 2, dtype=jnp.float32) / B_QK)
    ang = positions.astype(jnp.float32)[..., None] * inv_freq
    return jnp.cos(ang), jnp.sin(ang)


def _apply_rope(t, cos, sin):
    tf = t.astype(jnp.float32)
    t1, t2 = jnp.split(tf, 2, axis=-1)
    c = cos[:, :, None, None, :]
    s = sin[:, :, None, None, :]
    return jnp.concatenate([t1 * c - t2 * s, t1 * s + t2 * c], axis=-1)


def _rwkv7_mixer(z, mix, w0, w2, a0, a2, g2, k_k, k_a, r_k, ln_w, ln_b):
    bsz, seq, _ = z.shape
    zf = z.astype(jnp.float32)
    z_prev = jnp.pad(zf, ((0, 0), (1, 0), (0, 0)))[:, :-1]
    zf = zf + (z_prev - zf) * mix
    splits = [A_DIM, 2 * A_DIM, 3 * A_DIM, 3 * A_DIM + DECAY_LORA, 3 * A_DIM + DECAY_LORA + AAA_LORA]
    r, k, v, zw, za, zg = jnp.split(zf, splits, axis=-1)
    w_log = -jax.nn.softplus(-(w0 + jnp.tanh(zw) @ w2)) - 0.5
    decay = jnp.exp(-jnp.exp(w_log))
    a = jax.nn.sigmoid(a0 + za @ a2)
    g = jax.nn.sigmoid(zg) @ g2
    heads = lambda t: t.reshape(bsz, seq, A_HEADS, A_HEAD)
    kk = heads(k * k_k)
    kk = kk / jnp.maximum(jnp.sqrt(jnp.sum(kk * kk, axis=-1, keepdims=True)), 1e-12)
    k = k * (1.0 + (a - 1.0) * k_a)
    r, k, v, a, decay = heads(r), heads(k), heads(v), heads(a), heads(decay)

    def step(state, inp):
        r_t, w_t, k_t, v_t, kk_t, a_t = inp
        sa = jnp.einsum('bhvk,bhk->bhv', state, -kk_t)
        state = (state * w_t[:, :, None, :]
                 + sa[..., None] * (kk_t * a_t)[:, :, None, :]
                 + v_t[..., None] * k_t[:, :, None, :])
        return state, jnp.einsum('bhvk,bhk->bhv', state, r_t)

    xs = tuple(jnp.moveaxis(t, 1, 0) for t in (r, decay, k, v, kk, a))
    state0 = jnp.zeros((bsz, A_HEADS, A_HEAD, A_HEAD), jnp.float32)
    _, y = lax.scan(step, state0, xs)
    y = jnp.moveaxis(y, 0, 1)
    mu = jnp.mean(y, axis=-1, keepdims=True)
    var = jnp.mean(jnp.square(y - mu), axis=-1, keepdims=True)
    y = ((y - mu) * lax.rsqrt(var + GN_EPS)).reshape(bsz, seq, A_DIM) * ln_w + ln_b
    bonus = jnp.sum(r * k * r_k, axis=-1, keepdims=True) * v
    y = (y + bonus.reshape(bsz, seq, A_DIM)) * g
    return y.astype(z.dtype)


def _diff_attention(z, cos, sin, lq1, lk1, lq2, lk2, subln, lam_init):
    bsz, seq, _ = z.shape
    q, k, v = jnp.split(z, [2 * B_HEADS * B_QK, 4 * B_HEADS * B_QK], axis=-1)
    q = _apply_rope(q.reshape(bsz, seq, B_HEADS, 2, B_QK), cos, sin)
    k = _apply_rope(k.reshape(bsz, seq, B_HEADS, 2, B_QK), cos, sin)
    v = v.reshape(bsz, seq, B_HEADS, B_V).astype(jnp.float32)
    lam = (jnp.exp(jnp.sum(lq1.astype(jnp.float32) * lk1.astype(jnp.float32)))
           - jnp.exp(jnp.sum(lq2.astype(jnp.float32) * lk2.astype(jnp.float32))) + lam_init)
    scale = B_QK ** -0.5
    key_chunk = jnp.arange(seq) // CHUNK

    def block(i):
        start = i * Q_BLOCK
        qb = lax.dynamic_slice_in_dim(q, start, Q_BLOCK, axis=1)
        s = jnp.einsum('bqhce,bkhce->bhcqk', qb, k) * scale
        q_chunk = (start + jnp.arange(Q_BLOCK)) // CHUNK
        mask = key_chunk[None, :] <= q_chunk[:, None]
        s = jnp.where(mask, s, -jnp.inf)
        p = jax.nn.softmax(s, axis=-1)
        attn = p[:, :, 0] - lam * p[:, :, 1]
        return jnp.einsum('bhqk,bkhv->bqhv', attn, v)

    o = lax.map(block, jnp.arange(seq // Q_BLOCK))
    o = jnp.moveaxis(o, 0, 1).reshape(bsz, seq, B_HEADS, B_V)
    o = o * lax.rsqrt(jnp.mean(o * o, axis=-1, keepdims=True) + EPS) * subln
    o = o * (1.0 - lam_init)
    return o.reshape(bsz, seq, B_DIM).astype(z.dtype)


def _complex_affine_combine(e1, e2):
    a1r, a1i, b1r, b1i = e1
    a2r, a2i, b2r, b2i = e2
    ar = a2r * a1r - a2i * a1i
    ai = a2r * a1i + a2i * a1r
    br = a2r * b1r - a2i * b1i + b2r
    bi = a2r * b1i + a2i * b1r + b2i
    return (ar, ai, br, bi)


def _s5_mixer(u, lam_re, lam_im, log_dt, b_re, b_im, c_re, c_im, d, glu_w, glu_b):
    bsz, seq, _ = u.shape
    uf = u.astype(jnp.float32).reshape(bsz, seq, C_GROUPS, C_GROUP)
    dt = jnp.exp(log_dt.astype(jnp.float32))[:, None]
    lr = lam_re.astype(jnp.float32)
    li = lam_im.astype(jnp.float32)
    er = jnp.exp(lr * dt)
    ab_re = er * jnp.cos(li * dt)
    ab_im = er * jnp.sin(li * dt)
    den = lr * lr + li * li
    nr = ab_re - 1.0
    f_re = (nr * lr + ab_im * li) / den
    f_im = (ab_im * lr - nr * li) / den
    br = b_re.astype(jnp.float32)
    bi = b_im.astype(jnp.float32)
    bb_re = f_re[..., None] * br - f_im[..., None] * bi
    bb_im = f_re[..., None] * bi + f_im[..., None] * br
    bu_re = jnp.einsum('gpc,bsgc->bsgp', bb_re, uf)
    bu_im = jnp.einsum('gpc,bsgc->bsgp', bb_im, uf)
    a_re = jnp.broadcast_to(ab_re[None, None], (1, seq, C_GROUPS, C_STATE))
    a_im = jnp.broadcast_to(ab_im[None, None], (1, seq, C_GROUPS, C_STATE))
    _, _, x_re, x_im = lax.associative_scan(_complex_affine_combine, (a_re, a_im, bu_re, bu_im), axis=1)
    y = (jnp.einsum('gcp,bsgp->bsgc', c_re.astype(jnp.float32), x_re)
         - jnp.einsum('gcp,bsgp->bsgc', c_im.astype(jnp.float32), x_im)
         + d.reshape(C_GROUPS, C_GROUP) * uf)
    y = jax.nn.gelu(y.reshape(bsz, seq, C_DIM))
    y = y * jax.nn.sigmoid(y @ glu_w + glu_b)
    return y.astype(u.dtype)


def _conv_glu_ffn(h, w_up, conv_w, w_down):
    seq = h.shape[1]
    val, gate = jnp.split(h @ w_up, 2, axis=-1)
    gp = jnp.pad(gate, ((0, 0), (CONV_W - 1, 0), (0, 0)))
    gate = sum(conv_w[j] * gp[:, j:j + seq] for j in range(CONV_W))
    return (jax.nn.gelu(gate) * val) @ w_down


def setup_inputs(seed: int = 0) -> dict:
    key = jax.random.key(seed)
    keys = jax.random.split(key, 48)
    counter = [0]

    def nk():
        counter[0] += 1
        return keys[counter[0] - 1]

    def nrm(shape, scale):
        return scale * jax.random.normal(nk(), shape, jnp.float32)

    L = DEPTH
    f32 = jnp.float32
    inp = {}
    inp['x'] = nrm((BATCH, SEQ, D_MODEL), 1.0)
    offset = jax.random.randint(nk(), (BATCH, 1), 0, 64, dtype=jnp.int32) * CHUNK
    inp['positions'] = (offset + jnp.arange(SEQ, dtype=jnp.int32)[None, :]).astype(jnp.int32)
    inp['norm_mix'] = 1.0 + nrm((L, D_MODEL), 0.02)
    inp['norm_ffn'] = 1.0 + nrm((L, D_MODEL), 0.02)
    inp['w_in'] = nrm((L, D_MODEL, N_IN), D_MODEL ** -0.5)
    inp['b_gate'] = nrm((L, N_BRANCH * D_MODEL), 0.01)
    inp['rw_mix'] = jax.random.uniform(nk(), (L, RW_IN), f32)
    inp['rw_w0'] = jnp.linspace(-6.0, -1.0, A_DIM, dtype=f32)[None, :] + nrm((L, A_DIM), 0.1)
    inp['rw_w2'] = nrm((L, DECAY_LORA, A_DIM), DECAY_LORA ** -0.5)
    inp['rw_a0'] = nrm((L, A_DIM), 0.1)
    inp['rw_a2'] = nrm((L, AAA_LORA, A_DIM), AAA_LORA ** -0.5)
    inp['rw_g2'] = nrm((L, GATE_LORA, A_DIM), GATE_LORA ** -0.5)
    inp['rw_k_k'] = 0.85 + nrm((L, A_DIM), 0.05)
    inp['rw_k_a'] = 1.0 + nrm((L, A_DIM), 0.05)
    inp['rw_r_k'] = nrm((L, A_HEADS, A_HEAD), 0.1)
    inp['rw_ln_w'] = 1.0 + nrm((L, A_DIM), 0.02)
    inp['rw_ln_b'] = nrm((L, A_DIM), 0.01)
    inp['da_lq1'] = nrm((L, B_QK), 0.1)
    inp['da_lk1'] = nrm((L, B_QK), 0.1)
    inp['da_lq2'] = nrm((L, B_QK), 0.1)
    inp['da_lk2'] = nrm((L, B_QK), 0.1)
    inp['da_subln'] = 1.0 + nrm((L, B_V), 0.02)
    inp['s5_lam_re'] = -0.5 + nrm((L, C_GROUPS, C_STATE), 0.01)
    inp['s5_lam_im'] = jnp.pi * jnp.arange(C_STATE, dtype=f32) + nrm((L, C_GROUPS, C_STATE), 0.01)
    inp['s5_log_dt'] = jax.random.uniform(nk(), (L, C_GROUPS), f32, math.log(1e-3), math.log(1e-1))
    inp['s5_b_re'] = nrm((L, C_GROUPS, C_STATE, C_GROUP), (2 * C_GROUP) ** -0.5)
    inp['s5_b_im'] = nrm((L, C_GROUPS, C_STATE, C_GROUP), (2 * C_GROUP) ** -0.5)
    inp['s5_c_re'] = nrm((L, C_GROUPS, C_GROUP, C_STATE), (2 * C_STATE) ** -0.5)
    inp['s5_c_im'] = nrm((L, C_GROUPS, C_GROUP, C_STATE), (2 * C_STATE) ** -0.5)
    inp['s5_d'] = nrm((L, C_DIM), 1.0)
    inp['s5_glu_w'] = nrm((L, C_DIM, C_DIM), C_DIM ** -0.5)
    inp['s5_glu_b'] = nrm((L, C_DIM), 0.01)
    inp['proj_a'] = nrm((L, A_DIM, D_MODEL), A_DIM ** -0.5)
    inp['proj_b'] = nrm((L, B_DIM, D_MODEL), B_DIM ** -0.5)
    inp['proj_c'] = nrm((L, C_DIM, D_MODEL), C_DIM ** -0.5)
    inp['w_out'] = nrm((L, D_MODEL, D_MODEL), D_MODEL ** -0.5)
    inp['ffn_up'] = nrm((L, D_MODEL, 2 * D_FF), D_MODEL ** -0.5)
    inp['ffn_conv'] = nrm((L, CONV_W, D_FF), CONV_W ** -0.5)
    inp['ffn_down'] = nrm((L, D_FF, D_MODEL), D_FF ** -0.5)
    inp['norm_final'] = 1.0 + nrm((D_MODEL,), 0.02)
    return inp


def reference(x, positions, norm_mix, norm_ffn, w_in, b_gate,
              rw_mix, rw_w0, rw_w2, rw_a0, rw_a2, rw_g2, rw_k_k, rw_k_a, rw_r_k, rw_ln_w, rw_ln_b,
              da_lq1, da_lk1, da_lq2, da_lk2, da_subln,
              s5_lam_re, s5_lam_im, s5_log_dt, s5_b_re, s5_b_im, s5_c_re, s5_c_im, s5_d, s5_glu_w, s5_glu_b,
              proj_a, proj_b, proj_c, w_out, ffn_up, ffn_conv, ffn_down, norm_final):
    bsz, seq, _ = x.shape
    cos, sin = _rope_tables(positions)
    for l in range(DEPTH):
        h = _rms_norm(x, norm_mix[l])
        z = h @ w_in[l]
        z_a, z_b, z_c, z_g = jnp.split(z, [RW_IN, RW_IN + B_IN, RW_IN + B_IN + C_DIM], axis=-1)
        y_a = _rwkv7_mixer(z_a, rw_mix[l], rw_w0[l], rw_w2[l], rw_a0[l], rw_a2[l], rw_g2[l],
                           rw_k_k[l], rw_k_a[l], rw_r_k[l], rw_ln_w[l], rw_ln_b[l])
        lam_init = 0.8 - 0.6 * math.exp(-0.3 * l)
        y_b = _diff_attention(z_b, cos, sin, da_lq1[l], da_lk1[l], da_lq2[l], da_lk2[l], da_subln[l], lam_init)
        y_c = _s5_mixer(z_c, s5_lam_re[l], s5_lam_im[l], s5_log_dt[l], s5_b_re[l], s5_b_im[l],
                        s5_c_re[l], s5_c_im[l], s5_d[l], s5_glu_w[l], s5_glu_b[l])
        gates = jax.nn.sigmoid(z_g + b_gate[l]).reshape(bsz, seq, N_BRANCH, D_MODEL)
        merged = (gates[:, :, 0] * (y_a @ proj_a[l])
                  + gates[:, :, 1] * (y_b @ proj_b[l])
                  + gates[:, :, 2] * (y_c @ proj_c[l]))
        x = x + (merged @ w_out[l]).astype(x.dtype)
        x = x + _conv_glu_ffn(_rms_norm(x, norm_ffn[l]), ffn_up[l], ffn_conv[l], ffn_down[l]).astype(x.dtype)
    return _rms_norm(x, norm_final)
```

```python
import functools
import math

import jax
import jax.numpy as jnp
import numpy as np
from jax import lax
from jax.experimental import pallas as pl
from jax.experimental.pallas import tpu as pltpu

F32 = jnp.float32
BF16 = jnp.bfloat16

D_MODEL = 2048
DEPTH = 4
CHUNK = 64
EPS = 1e-6
ROPE_THETA = 10000.0

A_HEAD = 64
A_DIM = 768
A_HEADS = 12
LORA_W = 96
LORA_A = 96
LORA_G = 256
LORA_PAD = 128
GN_EPS = 64e-5

B_QK = 64
B_V = 128
B_DIM = 768
B_HEADS = 6

C_GROUP = 16
C_DIM = 512
C_GROUPS = 32
C_STATE = 64
C_NS = C_GROUPS * C_STATE

N_BRANCH = 3
D_FF = 5632

Z_GATE = 0
Z_RKV = N_BRANCH * D_MODEL
Z_QKV = Z_RKV + 3 * A_DIM
Z_LORA = Z_QKV + 3 * B_DIM
LORA_IN = 2 * LORA_PAD + LORA_G
Z_U = Z_LORA + LORA_IN
Z_COLS = Z_U + C_DIM

VMEM_LIMIT = 56 * 1024 * 1024
NEG_BIG = -1e30


def _cparams(sem):
    return pltpu.CompilerParams(dimension_semantics=sem, vmem_limit_bytes=VMEM_LIMIT)


def _gelu_tanh(x):
    return 0.5 * x * (1.0 + jnp.tanh(math.sqrt(2.0 / math.pi) * (x + 0.044715 * (x * x * x))))


def _dot(a, b):
    return jnp.dot(a.astype(BF16), b.astype(BF16), preferred_element_type=F32)


def _dot_f32(a, b):
    return jnp.dot(a, b, preferred_element_type=F32, precision=lax.Precision.HIGHEST)


def _dot_nt(a, b):
    return lax.dot_general(a.astype(BF16), b.astype(BF16), (((1,), (1,)), ((), ())),
                           preferred_element_type=F32)


def _dot_tn(a, b, precision=None):
    return lax.dot_general(a, b, (((0,), (0,)), ((), ())), preferred_element_type=F32,
                           precision=precision)


def _norm_matmul_body(x_ref, g_ref, w_ref, o_ref, h_ref):
    @pl.when(pl.program_id(1) == 0)
    def _():
        x = x_ref[...]
        ms = jnp.mean(x * x, axis=-1, keepdims=True)
        h_ref[...] = (x * lax.rsqrt(ms + EPS) * g_ref[...]).astype(BF16)

    o_ref[...] = jnp.dot(h_ref[...], w_ref[...], preferred_element_type=F32).astype(o_ref.dtype)


def _norm_matmul(x, g, w, tm, tn):
    t, d = x.shape
    n = w.shape[1]
    return pl.pallas_call(
        _norm_matmul_body,
        out_shape=jax.ShapeDtypeStruct((t, n), F32),
        grid=(t // tm, n // tn),
        in_specs=[pl.BlockSpec((tm, d), lambda i, j: (i, 0)),
                  pl.BlockSpec((1, d), lambda i, j: (0, 0)),
                  pl.BlockSpec((d, tn), lambda i, j: (0, j))],
        out_specs=pl.BlockSpec((tm, tn), lambda i, j: (i, j)),
        scratch_shapes=[pltpu.VMEM((tm, d), BF16)],
        compiler_params=_cparams(("parallel", "arbitrary")),
        name="norm_in_proj",
    )(x, g, w)


def _final_norm_body(x_ref, g_ref, o_ref):
    x = x_ref[...]
    ms = jnp.mean(x * x, axis=-1, keepdims=True)
    o_ref[...] = x * lax.rsqrt(ms + EPS) * g_ref[...]


def _final_norm(x, g, tm):
    t, d = x.shape
    return pl.pallas_call(
        _final_norm_body,
        out_shape=jax.ShapeDtypeStruct((t, d), F32),
        grid=(t // tm,),
        in_specs=[pl.BlockSpec((tm, d), lambda i: (i, 0)),
                  pl.BlockSpec((1, d), lambda i: (0, 0))],
        out_specs=pl.BlockSpec((tm, d), lambda i: (i, 0)),
        compiler_params=_cparams(("parallel",)),
        name="final_norm",
    )(x, g)


RW_LANES = 2 * A_HEAD
RW_ROWS = 2 * CHUNK


def _shift_rows(x, prev_row):
    row = lax.broadcasted_iota(jnp.int32, x.shape, 0)
    return jnp.where(row == 0, prev_row, pltpu.roll(x, 1, 0))


def _rwkv_body(zr_ref, zk_ref, zv_ref, zl_ref,
               mr_ref, mk_ref, mv_ref, ml_ref,
               w0_ref, w2_ref, a0_ref, a2_ref, g2_ref,
               kk_ref, ka_ref, rk_ref, lnw_ref, lnb_ref,
               o_ref,
               st_ref, pr_ref, pk_ref, pv_ref, pl_ref):
    c = pl.program_id(2)

    @pl.when(c == 0)
    def _():
        st_ref[...] = jnp.zeros_like(st_ref)
        pr_ref[...] = jnp.zeros_like(pr_ref)
        pk_ref[...] = jnp.zeros_like(pk_ref)
        pv_ref[...] = jnp.zeros_like(pv_ref)
        pl_ref[...] = jnp.zeros_like(pl_ref)

    def token_shift(z_ref, prev_ref, mix_ref):
        z = z_ref[...]
        zp = _shift_rows(z, prev_ref[0:1, :])
        prev_ref[0:1, :] = z[CHUNK - 1:CHUNK, :]
        return z + (zp - z) * mix_ref[...]

    r = token_shift(zr_ref, pr_ref, mr_ref)
    k = token_shift(zk_ref, pk_ref, mk_ref)
    v = token_shift(zv_ref, pv_ref, mv_ref)
    zl = token_shift(zl_ref, pl_ref, ml_ref)
    zw = zl[:, 0:LORA_PAD]
    za = zl[:, LORA_PAD:2 * LORA_PAD]
    zg = zl[:, 2 * LORA_PAD:]

    pre_w = w0_ref[...] + _dot_f32(jnp.tanh(zw), w2_ref[...])
    neg = -pre_w
    softplus = jnp.maximum(neg, 0.0) + jnp.log(1.0 + jnp.exp(-jnp.abs(neg)))
    logw = -jnp.exp(-softplus - 0.5)
    a = jax.nn.sigmoid(a0_ref[...] + _dot_f32(za, a2_ref[...]))
    g = _dot_f32(jax.nn.sigmoid(zg), g2_ref[...])

    lane_r = lax.broadcasted_iota(jnp.int32, (RW_LANES, RW_LANES), 0)
    lane_c = lax.broadcasted_iota(jnp.int32, (RW_LANES, RW_LANES), 1)
    same_head = (lane_r // A_HEAD) == (lane_c // A_HEAD)
    head_ones = same_head.astype(F32)

    kk = k * kk_ref[...]
    kk_ss = _dot_f32(kk * kk, head_ones)
    kk = kk / jnp.maximum(jnp.sqrt(kk_ss), 1e-12)
    k = k * (1.0 + (a - 1.0) * ka_ref[...])

    ti = lax.broadcasted_iota(jnp.int32, (CHUNK, CHUNK), 0)
    tj = lax.broadcasted_iota(jnp.int32, (CHUNK, CHUNK), 1)
    lg = _dot_f32((tj <= ti).astype(F32), logw)
    lg_last = lg[CHUNK - 1:CHUNK, :]

    e_pos = jnp.exp(lg)
    e_neg = jnp.exp(-lg)
    a_t = -kk * jnp.exp(lg - logw)
    b_t = kk * a * e_neg
    k_t = k * e_neg
    r_t = r * e_pos
    e_end = jnp.exp(lg_last - lg)
    k_end = k * e_end
    b_end = kk * a * e_end

    lane = lax.broadcasted_iota(jnp.int32, (CHUNK, RW_LANES), 1)
    head0 = lane < A_HEAD

    def stack(x):
        return jnp.concatenate([jnp.where(head0, x, 0.0), jnp.where(head0, 0.0, x)], axis=0)

    a_s, b_s, k_s, r_s, v_s = stack(a_t), stack(b_t), stack(k_t), stack(r_t), stack(v)

    sr = lax.broadcasted_iota(jnp.int32, (RW_ROWS, RW_ROWS), 0)
    sc = lax.broadcasted_iota(jnp.int32, (RW_ROWS, RW_ROWS), 1)
    same_blk = (sr // CHUNK) == (sc // CHUNK)
    strict = same_blk & ((sr % CHUNK) > (sc % CHUNK))
    incl = same_blk & ((sr % CHUNK) >= (sc % CHUNK))

    l_ab = jnp.where(strict, _dot_nt(a_s, b_s), 0.0)
    l_ak = jnp.where(strict, _dot_nt(a_s, k_s), 0.0)
    l_rk = jnp.where(incl, _dot_nt(r_s, k_s), 0.0)
    l_rb = jnp.where(incl, _dot_nt(r_s, b_s), 0.0)

    eye = (sr == sc).astype(F32)
    s = 1
    t_inv = eye + jnp.where(((sr // 2) == (sc // 2)) & ((sr % 2) == 1) & ((sc % 2) == 0), l_ab, 0.0)
    s = 2
    while s < CHUNK:
        m = ((sr // (2 * s)) == (sc // (2 * s))) & (((sr // s) % 2) == 1) & (((sc // s) % 2) == 0)
        lm = jnp.where(m, l_ab, 0.0)
        t_inv = t_inv + _dot_f32(t_inv, _dot_f32(lm, t_inv))
        s *= 2

    st = st_ref[...]
    x_s = _dot(a_s, st) + _dot(l_ak, v_s)
    u_s = _dot(t_inv, x_s)
    y_s = _dot(r_s, st) + _dot(l_rk, v_s) + _dot(l_rb, u_s)
    y = y_s[0:CHUNK, :] + y_s[CHUNK:, :]
    u = u_s[0:CHUNK, :] + u_s[CHUNK:, :]

    decay_col = jnp.exp(_dot_tn(logw, jnp.ones((CHUNK, RW_LANES), F32), lax.Precision.HIGHEST))
    kcat = jnp.concatenate([k_end, b_end], axis=0).astype(BF16)
    vcat = jnp.concatenate([v, u], axis=0).astype(BF16)
    st_ref[...] = decay_col * st + jnp.where(same_head, _dot_tn(kcat, vcat), 0.0)

    head_avg = head_ones * (1.0 / A_HEAD)
    mu = _dot_f32(y, head_avg)
    d = y - mu
    var = _dot_f32(d * d, head_avg)
    yn = d * lax.rsqrt(var + GN_EPS) * lnw_ref[...] + lnb_ref[...]
    bonus = _dot_f32(r * k * rk_ref[...], head_ones) * v
    o_ref[0] = ((yn + bonus) * g).astype(o_ref.dtype)


def _rwkv_mixer(z, p, bsz, seq):
    nchunk = seq // CHUNK
    npair = A_HEADS // 2
    rblk, kblk, vblk = Z_RKV // RW_LANES, (Z_RKV + A_DIM) // RW_LANES, (Z_RKV + 2 * A_DIM) // RW_LANES
    lblk = Z_LORA // LORA_IN

    def zspec(width, blk0, per_pair):
        if per_pair:
            return pl.BlockSpec((CHUNK, width), lambda b, h, c: (b * nchunk + c, blk0 + h))
        return pl.BlockSpec((CHUNK, width), lambda b, h, c: (b * nchunk + c, blk0))

    def pair_vec():
        return pl.BlockSpec((1, RW_LANES), lambda b, h, c: (0, h))

    def pair_mat(rows):
        return pl.BlockSpec((rows, RW_LANES), lambda b, h, c: (0, h))

    return pl.pallas_call(
        _rwkv_body,
        out_shape=jax.ShapeDtypeStruct((bsz, seq, A_DIM), BF16),
        grid=(bsz, npair, nchunk),
        in_specs=[zspec(RW_LANES, rblk, True), zspec(RW_LANES, kblk, True), zspec(RW_LANES, vblk, True),
                  zspec(LORA_IN, lblk, False),
                  pair_vec(), pair_vec(), pair_vec(),
                  pl.BlockSpec((1, LORA_IN), lambda b, h, c: (0, 0)),
                  pair_vec(), pair_mat(LORA_PAD), pair_vec(), pair_mat(LORA_PAD), pair_mat(LORA_G),
                  pair_vec(), pair_vec(), pair_vec(), pair_vec(), pair_vec()],
        out_specs=pl.BlockSpec((1, CHUNK, RW_LANES), lambda b, h, c: (b, c, h)),
        scratch_shapes=[pltpu.VMEM((RW_LANES, RW_LANES), F32),
                        pltpu.VMEM((8, RW_LANES), F32), pltpu.VMEM((8, RW_LANES), F32),
                        pltpu.VMEM((8, RW_LANES), F32), pltpu.VMEM((8, LORA_IN), F32)],
        compiler_params=_cparams(("parallel", "parallel", "arbitrary")),
        name="rwkv7_mixer",
    )(z, z, z, z,
      p["mix_r"], p["mix_k"], p["mix_v"], p["mix_l"],
      p["w0"], p["w2"], p["a0"], p["a2"], p["g2"],
      p["k_k"], p["k_a"], p["r_k"], p["ln_w"], p["ln_b"])


def _rope_body(zq_ref, zk_ref, zv_ref, cos_ref, sin_ref, q1_ref, q2_ref, k_ref, v_ref):
    reps = B_DIM // 128
    cos = jnp.concatenate([cos_ref[...]] * reps, axis=1)
    sin = jnp.concatenate([sin_ref[...]] * reps, axis=1)
    lane = lax.broadcasted_iota(jnp.int32, cos.shape, 1)
    first_half = (lane % B_QK) < (B_QK // 2)
    comp0 = (lane % B_V) < B_QK

    def rope(x):
        partner = jnp.where(first_half, pltpu.roll(x, B_DIM - B_QK // 2, 1), pltpu.roll(x, B_QK // 2, 1))
        return x * cos + partner * sin

    q = rope(zq_ref[...]) * (B_QK ** -0.5)
    q1_ref[...] = jnp.where(comp0, q, 0.0).astype(BF16)
    q2_ref[...] = jnp.where(comp0, 0.0, q).astype(BF16)
    k_ref[...] = rope(zk_ref[...]).astype(BF16)
    v_ref[...] = zv_ref[...].astype(BF16)


def _rope_prep(z, cos_t, sin_t, tm):
    t = z.shape[0]
    qblk = Z_QKV // B_DIM
    out = jax.ShapeDtypeStruct((t, B_DIM), BF16)
    ospec = pl.BlockSpec((tm, B_DIM), lambda i: (i, 0))
    return pl.pallas_call(
        _rope_body,
        out_shape=(out, out, out, out),
        grid=(t // tm,),
        in_specs=[pl.BlockSpec((tm, B_DIM), lambda i: (i, qblk)),
                  pl.BlockSpec((tm, B_DIM), lambda i: (i, qblk + 1)),
                  pl.BlockSpec((tm, B_DIM), lambda i: (i, qblk + 2)),
                  pl.BlockSpec((tm, 128), lambda i: (i, 0)),
                  pl.BlockSpec((tm, 128), lambda i: (i, 0))],
        out_specs=(ospec, ospec, ospec, ospec),
        compiler_params=_cparams(("parallel",)),
        name="rope_prep",
    )(z, z, z, cos_t, sin_t)


def _attn_body(q1_ref, q2_ref, k_ref, v_ref, lam_ref, sub_ref, o_ref,
               m1_ref, l1_ref, a1_ref, m2_ref, l2_ref, a2_ref, *, tq):
    i = pl.program_id(2)
    j = pl.program_id(3)

    @pl.when(j == 0)
    def _():
        m1_ref[...] = jnp.full_like(m1_ref, NEG_BIG)
        m2_ref[...] = jnp.full_like(m2_ref, NEG_BIG)
        l1_ref[...] = jnp.zeros_like(l1_ref)
        l2_ref[...] = jnp.zeros_like(l2_ref)
        a1_ref[...] = jnp.zeros_like(a1_ref)
        a2_ref[...] = jnp.zeros_like(a2_ref)

    def update(masked):
        kb = k_ref[0]
        vb = v_ref[0]
        if masked:
            row = lax.broadcasted_iota(jnp.int32, (tq, tq), 0)
            col = lax.broadcasted_iota(jnp.int32, (tq, tq), 1)
            visible = (col // CHUNK) <= (row // CHUNK)
        for q_ref, m_ref, l_ref, a_ref in ((q1_ref, m1_ref, l1_ref, a1_ref),
                                           (q2_ref, m2_ref, l2_ref, a2_ref)):
            s = lax.dot_general(q_ref[0], kb, (((1,), (1,)), ((), ())), preferred_element_type=F32)
            if masked:
                s = jnp.where(visible, s, NEG_BIG)
            m_old = m_ref[...]
            m_new = jnp.maximum(m_old, jnp.max(s, axis=-1, keepdims=True))
            alpha = jnp.exp(m_old - m_new)
            p = jnp.exp(s - m_new)
            l_ref[...] = alpha * l_ref[...] + jnp.sum(p, axis=-1, keepdims=True)
            a_ref[...] = alpha * a_ref[...] + jnp.dot(p.astype(BF16), vb, preferred_element_type=F32)
            m_ref[...] = m_new

    @pl.when(j < i)
    def _():
        update(False)

    @pl.when(j == i)
    def _():
        update(True)
        o = a1_ref[...] / l1_ref[...] - lam_ref[...] * (a2_ref[...] / l2_ref[...])
        o = o * lax.rsqrt(jnp.mean(o * o, axis=-1, keepdims=True) + EPS) * sub_ref[...]
        o_ref[0] = o.astype(o_ref.dtype)


def _diff_attention(q1, q2, k, v, lam, sub, tq):
    bsz, seq, _ = q1.shape
    nq = seq // tq
    qspec = pl.BlockSpec((1, tq, B_V), lambda b, h, i, j: (b, i, h))
    kspec = pl.BlockSpec((1, tq, B_V), lambda b, h, i, j: (b, jnp.minimum(j, i), h))
    pspec = pl.BlockSpec((1, B_V), lambda b, h, i, j: (0, 0))
    return pl.pallas_call(
        functools.partial(_attn_body, tq=tq),
        out_shape=jax.ShapeDtypeStruct((bsz, seq, B_DIM), BF16),
        grid=(bsz, B_HEADS, nq, nq),
        in_specs=[qspec, qspec, kspec, kspec, pspec, pspec],
        out_specs=pl.BlockSpec((1, tq, B_V), lambda b, h, i, j: (b, i, h)),
        scratch_shapes=[pltpu.VMEM((tq, 1), F32), pltpu.VMEM((tq, 1), F32), pltpu.VMEM((tq, B_V), F32),
                        pltpu.VMEM((tq, 1), F32), pltpu.VMEM((tq, 1), F32), pltpu.VMEM((tq, B_V), F32)],
        compiler_params=_cparams(("parallel", "parallel", "parallel", "arbitrary")),
        name="diff_attention",
    )(q1, q2, k, v, lam, sub)


S5_SUB = 8


def _s5_body(u_ref, bd_ref, cd_ref, pw_ref, d_ref, gw_ref, gb_ref, o_ref, xs_ref, carry_ref, *, lt):
    t = pl.program_id(1)

    @pl.when(t == 0)
    def _():
        carry_ref[...] = jnp.zeros_like(carry_ref)

    u = u_ref[...]
    bu = _dot(u, bd_ref[...])
    re = bu[:, :C_NS]
    im = bu[:, C_NS:]

    row = lax.broadcasted_iota(jnp.int32, (lt, C_NS), 0) % S5_SUB
    for d in (1, 2, 4):
        pr = pw_ref[d - 1:d, :C_NS]
        pi = pw_ref[d - 1:d, C_NS:]
        keep = row >= d
        sre = jnp.where(keep, pltpu.roll(re, d, 0), 0.0)
        sim = jnp.where(keep, pltpu.roll(im, d, 0), 0.0)
        re, im = re + pr * sre - pi * sim, im + pr * sim + pi * sre
    xs_ref[:, :C_NS] = re
    xs_ref[:, C_NS:] = im

    pw_re = pw_ref[:, :C_NS]
    pw_im = pw_ref[:, C_NS:]

    def group(gidx, carry):
        cr, ci = carry
        rows = pl.ds(pl.multiple_of(gidx * S5_SUB, S5_SUB), S5_SUB)
        nre = xs_ref[rows, :C_NS] + pw_re * cr - pw_im * ci
        nim = xs_ref[rows, C_NS:] + pw_re * ci + pw_im * cr
        xs_ref[rows, :C_NS] = nre
        xs_ref[rows, C_NS:] = nim
        return nre[S5_SUB - 1:S5_SUB, :], nim[S5_SUB - 1:S5_SUB, :]

    cr, ci = lax.fori_loop(0, lt // S5_SUB, group, (carry_ref[0:1, :C_NS], carry_ref[0:1, C_NS:]))
    carry_ref[0:1, :C_NS] = cr
    carry_ref[0:1, C_NS:] = ci

    y = _dot(xs_ref[...], cd_ref[...]) + d_ref[...] * u
    y = _gelu_tanh(y)
    y = y * jax.nn.sigmoid(_dot(y, gw_ref[...]) + gb_ref[...])
    o_ref[...] = y.astype(o_ref.dtype)


def _s5_mixer(z, p, bsz, seq, lt):
    nt = seq // lt
    ublk = Z_U // C_DIM
    full = lambda shape: pl.BlockSpec(shape, lambda b, t: (0, 0))
    return pl.pallas_call(
        functools.partial(_s5_body, lt=lt),
        out_shape=jax.ShapeDtypeStruct((bsz * seq, C_DIM), BF16),
        grid=(bsz, nt),
        in_specs=[pl.BlockSpec((lt, C_DIM), lambda b, t: (b * nt + t, ublk)),
                  full((C_DIM, 2 * C_NS)), full((2 * C_NS, C_DIM)), full((S5_SUB, 2 * C_NS)),
                  full((1, C_DIM)), full((C_DIM, C_DIM)), full((1, C_DIM))],
        out_specs=pl.BlockSpec((lt, C_DIM), lambda b, t: (b * nt + t, 0)),
        scratch_shapes=[pltpu.VMEM((lt, 2 * C_NS), F32), pltpu.VMEM((S5_SUB, 2 * C_NS), F32)],
        compiler_params=_cparams(("parallel", "arbitrary")),
        name="s5_mixer",
    )(z, p["bd"], p["cd"], p["pw"], p["d"], p["glu_w"], p["glu_b"])


def _merge_body(ya_ref, yb_ref, yc_ref, g0_ref, g1_ref, g2_ref, bg_ref, pa_ref, pb_ref, pc_ref, o_ref):
    bg = bg_ref[...]
    m = jax.nn.sigmoid(g0_ref[...] + bg[:, 0:D_MODEL]) * jnp.dot(
        ya_ref[...], pa_ref[...], preferred_element_type=F32)
    m = m + jax.nn.sigmoid(g1_ref[...] + bg[:, D_MODEL:2 * D_MODEL]) * jnp.dot(
        yb_ref[...], pb_ref[...], preferred_element_type=F32)
    m = m + jax.nn.sigmoid(g2_ref[...] + bg[:, 2 * D_MODEL:]) * jnp.dot(
        yc_ref[...], pc_ref[...], preferred_element_type=F32)
    o_ref[...] = m.astype(o_ref.dtype)


def _merge(ya, yb, yc, z, bg, pa, pb, pc, tm):
    t = ya.shape[0]
    full = lambda shape: pl.BlockSpec(shape, lambda i: (0, 0))
    return pl.pallas_call(
        _merge_body,
        out_shape=jax.ShapeDtypeStruct((t, D_MODEL), BF16),
        grid=(t // tm,),
        in_specs=[pl.BlockSpec((tm, A_DIM), lambda i: (i, 0)),
                  pl.BlockSpec((tm, B_DIM), lambda i: (i, 0)),
                  pl.BlockSpec((tm, C_DIM), lambda i: (i, 0)),
                  pl.BlockSpec((tm, D_MODEL), lambda i: (i, 0)),
                  pl.BlockSpec((tm, D_MODEL), lambda i: (i, 1)),
                  pl.BlockSpec((tm, D_MODEL), lambda i: (i, 2)),
                  full((1, N_BRANCH * D_MODEL)),
                  full((A_DIM, D_MODEL)), full((B_DIM, D_MODEL)), full((C_DIM, D_MODEL))],
        out_specs=pl.BlockSpec((tm, D_MODEL), lambda i: (i, 0)),
        compiler_params=_cparams(("parallel",)),
        name="gated_merge",
    )(ya, yb, yc, z, z, z, bg, pa, pb, pc)


def _matmul_res_body(a_ref, w_ref, r_ref, o_ref):
    o_ref[...] = r_ref[...] + jnp.dot(a_ref[...], w_ref[...], preferred_element_type=F32)


def _matmul_res(a, w, res, tm):
    t, kdim = a.shape
    n = w.shape[1]
    return pl.pallas_call(
        _matmul_res_body,
        out_shape=jax.ShapeDtypeStruct((t, n), F32),
        grid=(t // tm,),
        in_specs=[pl.BlockSpec((tm, kdim), lambda i: (i, 0)),
                  pl.BlockSpec((kdim, n), lambda i: (0, 0)),
                  pl.BlockSpec((tm, n), lambda i: (i, 0))],
        out_specs=pl.BlockSpec((tm, n), lambda i: (i, 0)),
        compiler_params=_cparams(("parallel",)),
        name="out_proj_residual",
    )(a, w, res)


FFN_HALO = 8


def _ffn_body(x_ref, xh_ref, g_ref, wv_ref, wg_ref, cw_ref, wd_ref, o_ref, h_ref, hh_ref, acc_ref,
              *, tm, blocks_per_seq):
    i = pl.program_id(0)
    j = pl.program_id(1)

    def norm(x):
        ms = jnp.mean(x * x, axis=-1, keepdims=True)
        return x * lax.rsqrt(ms + EPS) * g_ref[...]

    @pl.when(j == 0)
    def _():
        h_ref[...] = norm(x_ref[...]).astype(BF16)
        seq_start = (i % blocks_per_seq) == 0
        hh_ref[...] = jnp.where(seq_start, 0.0, norm(xh_ref[...])).astype(BF16)
        acc_ref[...] = jnp.zeros_like(acc_ref)

    h = h_ref[...]
    val = jnp.dot(h, wv_ref[...], preferred_element_type=F32)
    gate = jnp.dot(h, wg_ref[...], preferred_element_type=F32)
    gate_halo = jnp.dot(hh_ref[...], wg_ref[...], preferred_element_type=F32)

    row = lax.broadcasted_iota(jnp.int32, gate.shape, 0)
    gm1 = jnp.where(row == 0, gate_halo[FFN_HALO - 1:FFN_HALO, :], pltpu.roll(gate, 1, 0))
    gm2 = jnp.where(row == 0, gate_halo[FFN_HALO - 2:FFN_HALO - 1, :],
                    jnp.where(row == 1, gate_halo[FFN_HALO - 1:FFN_HALO, :], pltpu.roll(gate, 2, 0)))
    cw = cw_ref[...]
    conv = cw[0:1, :] * gm2 + cw[1:2, :] * gm1 + cw[2:3, :] * gate
    act = (_gelu_tanh(conv) * val).astype(BF16)
    acc_ref[...] += jnp.dot(act, wd_ref[...], preferred_element_type=F32)

    @pl.when(j == pl.num_programs(1) - 1)
    def _():
        o_ref[...] = x_ref[...] + acc_ref[...]


def _ffn(x, g, w_up, conv_w, w_down, seq, tm, tn):
    t, d = x.shape
    nff = D_FF // tn
    halo_per_block = tm // FFN_HALO
    return pl.pallas_call(
        functools.partial(_ffn_body, tm=tm, blocks_per_seq=seq // tm),
        out_shape=jax.ShapeDtypeStruct((t, d), F32),
        grid=(t // tm, nff),
        in_specs=[pl.BlockSpec((tm, d), lambda i, j: (i, 0)),
                  pl.BlockSpec((FFN_HALO, d), lambda i, j: (jnp.maximum(i * halo_per_block - 1, 0), 0)),
                  pl.BlockSpec((1, d), lambda i, j: (0, 0)),
                  pl.BlockSpec((d, tn), lambda i, j: (0, j)),
                  pl.BlockSpec((d, tn), lambda i, j: (0, nff + j)),
                  pl.BlockSpec((3, tn), lambda i, j: (0, j)),
                  pl.BlockSpec((tn, d), lambda i, j: (j, 0))],
        out_specs=pl.BlockSpec((tm, d), lambda i, j: (i, 0)),
        scratch_shapes=[pltpu.VMEM((tm, d), BF16), pltpu.VMEM((FFN_HALO, d), BF16),
                        pltpu.VMEM((tm, d), F32)],
        compiler_params=_cparams(("parallel", "arbitrary")),
        name="conv_glu_ffn",
    )(x, x, g, w_up, w_up, conv_w, w_down)


RW_IN = 3 * A_DIM + LORA_W + LORA_A + LORA_G
B_IN = 3 * B_DIM


def _pad_cols(w, width):
    return jnp.pad(w, ((0, 0), (0, width - w.shape[1])))


def _layout_w_in(w):
    o = 0
    rkv = w[:, o:o + 3 * A_DIM]; o += 3 * A_DIM
    zw = w[:, o:o + LORA_W]; o += LORA_W
    za = w[:, o:o + LORA_A]; o += LORA_A
    zg = w[:, o:o + LORA_G]; o += LORA_G
    qkv = w[:, o:o + B_IN]; o += B_IN
    u = w[:, o:o + C_DIM]; o += C_DIM
    gates = w[:, o:]
    out = jnp.concatenate([gates, rkv, qkv, _pad_cols(zw, LORA_PAD), _pad_cols(za, LORA_PAD), zg, u], axis=1)
    return out.astype(BF16)


def _rwkv_params(mix, w0, w2, a0, a2, g2, k_k, k_a, r_k, ln_w, ln_b):
    row = lambda v: v.reshape(1, -1)
    o = 3 * A_DIM
    mix_l = jnp.concatenate([
        jnp.pad(mix[o:o + LORA_W], (0, LORA_PAD - LORA_W)),
        jnp.pad(mix[o + LORA_W:o + LORA_W + LORA_A], (0, LORA_PAD - LORA_A)),
        mix[o + LORA_W + LORA_A:]])
    pad_rows = lambda m: jnp.pad(m, ((0, LORA_PAD - m.shape[0]), (0, 0)))
    return dict(mix_r=row(mix[0:A_DIM]), mix_k=row(mix[A_DIM:2 * A_DIM]), mix_v=row(mix[2 * A_DIM:3 * A_DIM]),
                mix_l=row(mix_l), w0=row(w0), w2=pad_rows(w2), a0=row(a0), a2=pad_rows(a2), g2=g2,
                k_k=row(k_k), k_a=row(k_a), r_k=row(r_k), ln_w=row(ln_w), ln_b=row(ln_b))


def _s5_params(lam_re, lam_im, log_dt, b_re, b_im, c_re, c_im, d, glu_w, glu_b):
    dt = jnp.exp(log_dt)[:, None]
    er = jnp.exp(lam_re * dt)
    ab_re = er * jnp.cos(lam_im * dt)
    ab_im = er * jnp.sin(lam_im * dt)
    den = lam_re * lam_re + lam_im * lam_im
    nr = ab_re - 1.0
    f_re = (nr * lam_re + ab_im * lam_im) / den
    f_im = (ab_im * lam_re - nr * lam_im) / den
    bb_re = f_re[..., None] * b_re - f_im[..., None] * b_im
    bb_im = f_re[..., None] * b_im + f_im[..., None] * b_re
    eye_g = jnp.eye(C_GROUPS, dtype=F32)
    bd_re = jnp.einsum("gpc,gh->gchp", bb_re, eye_g).reshape(C_DIM, C_NS)
    bd_im = jnp.einsum("gpc,gh->gchp", bb_im, eye_g).reshape(C_DIM, C_NS)
    cd_re = jnp.einsum("gcp,gh->gphc", c_re, eye_g).reshape(C_NS, C_DIM)
    cd_im = jnp.einsum("gcp,gh->gphc", c_im, eye_g).reshape(C_NS, C_DIM)
    pr, pi = [ab_re.reshape(-1)], [ab_im.reshape(-1)]
    for _ in range(S5_SUB - 1):
        pr.append(pr[-1] * pr[0] - pi[-1] * pi[0])
        pi.append(pr[-2] * pi[0] + pi[-1] * pr[0])
    pw = jnp.concatenate([jnp.stack(pr), jnp.stack(pi)], axis=1)
    return dict(bd=jnp.concatenate([bd_re, bd_im], axis=1).astype(BF16),
                cd=jnp.concatenate([cd_re, -cd_im], axis=0).astype(BF16),
                pw=pw, d=d.reshape(1, -1), glu_w=glu_w.astype(BF16), glu_b=glu_b.reshape(1, -1))


def kernel(x, positions, norm_mix, norm_ffn, w_in, b_gate, rw_mix, rw_w0, rw_w2, rw_a0, rw_a2, rw_g2, rw_k_k, rw_k_a, rw_r_k, rw_ln_w, rw_ln_b, da_lq1, da_lk1, da_lq2, da_lk2, da_subln, s5_lam_re, s5_lam_im, s5_log_dt, s5_b_re, s5_b_im, s5_c_re, s5_c_im, s5_d, s5_glu_w, s5_glu_b, proj_a, proj_b, proj_c, w_out, ffn_up, ffn_conv, ffn_down, norm_final):
    bsz, seq, d = x.shape
    t = bsz * seq
    tm_proj = min(1024, seq)
    tm_ffn = min(512, seq)
    tm_small = min(256, seq)
    tq = min(256, seq)
    lt = min(128, seq)

    inv_freq = ROPE_THETA ** (-jnp.arange(0, B_QK, 2, dtype=F32) / B_QK)
    ang = positions.astype(F32)[..., None] * inv_freq
    cos, sin = jnp.cos(ang), jnp.sin(ang)
    cos_t = jnp.concatenate([cos, cos, cos, cos], axis=-1).reshape(t, 128)
    sin_t = jnp.concatenate([-sin, sin, -sin, sin], axis=-1).reshape(t, 128)

    xf = x.reshape(t, d)
    for l in range(DEPTH):
        z = _norm_matmul(xf, norm_mix[l].reshape(1, d), _layout_w_in(w_in[l]), tm_proj, 512)

        rp = _rwkv_params(rw_mix[l], rw_w0[l], rw_w2[l], rw_a0[l], rw_a2[l], rw_g2[l], rw_k_k[l],
                          rw_k_a[l], rw_r_k[l].reshape(-1), rw_ln_w[l], rw_ln_b[l])
        y_a = _rwkv_mixer(z, rp, bsz, seq).reshape(t, A_DIM)

        lam_init = 0.8 - 0.6 * math.exp(-0.3 * l)
        lam = jnp.exp(jnp.sum(da_lq1[l] * da_lk1[l])) - jnp.exp(jnp.sum(da_lq2[l] * da_lk2[l])) + lam_init
        q1, q2, kr, vr = _rope_prep(z, cos_t, sin_t, tm_ffn)
        shp = (bsz, seq, B_DIM)
        y_b = _diff_attention(q1.reshape(shp), q2.reshape(shp), kr.reshape(shp), vr.reshape(shp),
                              jnp.full((1, B_V), lam, F32),
                              (da_subln[l] * (1.0 - lam_init)).reshape(1, B_V), tq).reshape(t, B_DIM)

        sp = _s5_params(s5_lam_re[l], s5_lam_im[l], s5_log_dt[l], s5_b_re[l], s5_b_im[l], s5_c_re[l],
                        s5_c_im[l], s5_d[l], s5_glu_w[l], s5_glu_b[l])
        y_c = _s5_mixer(z, sp, bsz, seq, lt)

        merged = _merge(y_a, y_b, y_c, z, b_gate[l].reshape(1, -1), proj_a[l].astype(BF16),
                        proj_b[l].astype(BF16), proj_c[l].astype(BF16), tm_small)
        xf = _matmul_res(merged, w_out[l].astype(BF16), xf, tm_ffn)
        xf = _ffn(xf, norm_ffn[l].reshape(1, d), ffn_up[l].astype(BF16), ffn_conv[l],
                  ffn_down[l].astype(BF16), seq, tm_ffn, 512)
    return _final_norm(xf, norm_final.reshape(1, d), tm_ffn).reshape(bsz, seq, d)
```

```python
import functools
import math

import jax
import jax.numpy as jnp
import numpy as np
from jax import lax
from jax.experimental import pallas as pl
from jax.experimental.pallas import tpu as pltpu

F32 = jnp.float32
BF16 = jnp.bfloat16

D_MODEL = 2048
DEPTH = 4
CHUNK = 64
EPS = 1e-6
ROPE_THETA = 10000.0

A_HEAD = 64
A_DIM = 768
A_HEADS = 12
LORA_W = 96
LORA_A = 96
LORA_G = 256
LORA_PAD = 128
GN_EPS = 64e-5

B_QK = 64
B_V = 128
B_DIM = 768
B_HEADS = 6

C_GROUP = 16
C_DIM = 512
C_GROUPS = 32
C_STATE = 64
C_NS = C_GROUPS * C_STATE

N_BRANCH = 3
D_FF = 5632

Z_GATE = 0
Z_RKV = N_BRANCH * D_MODEL
Z_QKV = Z_RKV + 3 * A_DIM
Z_LORA = Z_QKV + 3 * B_DIM
LORA_IN = 2 * LORA_PAD + LORA_G
Z_U = Z_LORA + LORA_IN
Z_COLS = Z_U + C_DIM

VMEM_LIMIT = 56 * 1024 * 1024
NEG_BIG = -1e30


def _cparams(sem):
    return pltpu.CompilerParams(dimension_semantics=sem, vmem_limit_bytes=VMEM_LIMIT)


def _gelu_tanh(x):
    return 0.5 * x * (1.0 + jnp.tanh(math.sqrt(2.0 / math.pi) * (x + 0.044715 * (x * x * x))))


def _dot(a, b):
    return jnp.dot(a.astype(BF16), b.astype(BF16), preferred_element_type=F32)


def _dot_f32(a, b):
    return jnp.dot(a, b, preferred_element_type=F32, precision=lax.Precision.HIGHEST)


def _dot_nt(a, b):
    return lax.dot_general(a.astype(BF16), b.astype(BF16), (((1,), (1,)), ((), ())),
                           preferred_element_type=F32)


def _dot_tn(a, b, precision=None):
    return lax.dot_general(a, b, (((0,), (0,)), ((), ())), preferred_element_type=F32,
                           precision=precision)


def _norm_matmul_body(x_ref, g_ref, w_ref, o_ref, h_ref):
    @pl.when(pl.program_id(1) == 0)
    def _():
        x = x_ref[...]
        ms = jnp.mean(x * x, axis=-1, keepdims=True)
        h_ref[...] = (x * lax.rsqrt(ms + EPS) * g_ref[...]).astype(BF16)

    o_ref[...] = jnp.dot(h_ref[...], w_ref[...], preferred_element_type=F32).astype(o_ref.dtype)


def _norm_matmul(x, g, w, tm, tn):
    t, d = x.shape
    n = w.shape[1]
    return pl.pallas_call(
        _norm_matmul_body,
        out_shape=jax.ShapeDtypeStruct((t, n), F32),
        grid=(t // tm, n // tn),
        in_specs=[pl.BlockSpec((tm, d), lambda i, j: (i, 0)),
                  pl.BlockSpec((1, d), lambda i, j: (0, 0)),
                  pl.BlockSpec((d, tn), lambda i, j: (0, j))],
        out_specs=pl.BlockSpec((tm, tn), lambda i, j: (i, j)),
        scratch_shapes=[pltpu.VMEM((tm, d), BF16)],
        compiler_params=_cparams(("parallel", "arbitrary")),
        name="norm_in_proj",
    )(x, g, w)


def _final_norm_body(x_ref, g_ref, o_ref):
    x = x_ref[...]
    ms = jnp.mean(x * x, axis=-1, keepdims=True)
    o_ref[...] = x * lax.rsqrt(ms + EPS) * g_ref[...]


def _final_norm(x, g, tm):
    t, d = x.shape
    return pl.pallas_call(
        _final_norm_body,
        out_shape=jax.ShapeDtypeStruct((t, d), F32),
        grid=(t // tm,),
        in_specs=[pl.BlockSpec((tm, d), lambda i: (i, 0)),
                  pl.BlockSpec((1, d), lambda i: (0, 0))],
        out_specs=pl.BlockSpec((tm, d), lambda i: (i, 0)),
        compiler_params=_cparams(("parallel",)),
        name="final_norm",
    )(x, g)


RW_LANES = 2 * A_HEAD
RW_ROWS = 2 * CHUNK
RW_PAIRS = A_HEADS // 2
RW_HALO = 8


def _shift_rows(x, prev_row):
    row = lax.broadcasted_iota(jnp.int32, x.shape, 0)
    return jnp.where(row == 0, prev_row, pltpu.roll(x, 1, 0))


def _split_bf16(x, pieces):
    out = []
    for _ in range(pieces - 1):
        h = x.astype(BF16)
        out.append(h)
        x = x - h.astype(F32)
    out.append(x.astype(BF16))
    return out


def _pair_masks():
    lane_r = lax.broadcasted_iota(jnp.int32, (RW_LANES, RW_LANES), 0)
    lane_c = lax.broadcasted_iota(jnp.int32, (RW_LANES, RW_LANES), 1)
    same_head = (lane_r // A_HEAD) == (lane_c // A_HEAD)
    head0 = lax.broadcasted_iota(jnp.int32, (CHUNK, RW_LANES), 1) < A_HEAD
    return same_head, head0


def _stack_heads(x, head0):
    return jnp.concatenate([jnp.where(head0, x, 0.0), jnp.where(head0, 0.0, x)], axis=0)


def _unstack_heads(x):
    return x[0:CHUNK, :] + x[CHUNK:, :]


def _rwkv_prep_body(zr_ref, zk_ref, zv_ref, zl_ref, hr_ref, hk_ref, hv_ref, hl_ref,
                    mr_ref, mk_ref, mv_ref, ml_ref,
                    w0_ref, w2h_ref, w2l_ref, a0_ref, a2_ref, g2_ref, kk_ref, ka_ref, rk_ref,
                    w_ref, u0_ref, rt_ref, y0_ref, lrb_ref, kend_ref, bend_ref, v_ref, gam_ref,
                    g_ref, bonus_ref):
    first = pl.program_id(1) == 0

    def token_shift(z_ref, halo_ref, mix_ref):
        z = z_ref[...]
        prev = jnp.where(first, 0.0, halo_ref[RW_HALO - 1:RW_HALO, :])
        return z + (_shift_rows(z, prev) - z) * mix_ref[...]

    r = token_shift(zr_ref, hr_ref, mr_ref)
    k = token_shift(zk_ref, hk_ref, mk_ref)
    v = token_shift(zv_ref, hv_ref, mv_ref)
    zl = token_shift(zl_ref, hl_ref, ml_ref)
    zw = zl[:, 0:LORA_PAD]
    za = zl[:, LORA_PAD:2 * LORA_PAD]
    zg = zl[:, 2 * LORA_PAD:]

    th_h, th_l = _split_bf16(jnp.tanh(zw), 2)
    w2h = w2h_ref[...]
    pre_w = (w0_ref[...] + jnp.dot(th_h, w2h, preferred_element_type=F32)
             + jnp.dot(th_h, w2l_ref[...], preferred_element_type=F32)
             + jnp.dot(th_l, w2h, preferred_element_type=F32))
    neg = -pre_w
    softplus = jnp.maximum(neg, 0.0) + jnp.log(1.0 + jnp.exp(-jnp.abs(neg)))
    logw = -jnp.exp(-softplus - 0.5)
    a = jax.nn.sigmoid(a0_ref[...] + _dot(za, a2_ref[...]))
    g = _dot(jax.nn.sigmoid(zg), g2_ref[...])
    g_ref[...] = g.astype(g_ref.dtype)
    v_ref[...] = v.astype(v_ref.dtype)

    ti = lax.broadcasted_iota(jnp.int32, (CHUNK, CHUNK), 0)
    tj = lax.broadcasted_iota(jnp.int32, (CHUNK, CHUNK), 1)
    tri = (tj <= ti).astype(BF16)
    lg = sum(jnp.dot(tri, piece, preferred_element_type=F32) for piece in _split_bf16(logw, 3))
    lg_last = lg[CHUNK - 1:CHUNK, :]
    gam_ref[0] = jnp.exp(lg_last)

    kk_raw = k * kk_ref[...]
    k = k * (1.0 + (a - 1.0) * ka_ref[...])
    e_pos = jnp.exp(lg)
    e_neg = jnp.exp(-lg)
    e_prev = jnp.exp(lg - logw)
    e_end = jnp.exp(lg_last - lg)
    rt_ref[...] = (r * e_pos).astype(rt_ref.dtype)
    kend_ref[...] = (k * e_end).astype(kend_ref.dtype)
    k_t = k * e_neg
    rkr = r * k * rk_ref[...]

    same_head, head0 = _pair_masks()
    head_ones = same_head.astype(BF16)
    sr = lax.broadcasted_iota(jnp.int32, (RW_ROWS, RW_ROWS), 0)
    sc = lax.broadcasted_iota(jnp.int32, (RW_ROWS, RW_ROWS), 1)
    same_blk = (sr // CHUNK) == (sc // CHUNK)
    strict = same_blk & ((sr % CHUNK) > (sc % CHUNK))
    incl = same_blk & ((sr % CHUNK) >= (sc % CHUNK))
    eye = (sr == sc).astype(F32)
    level_masks = []
    s = 1
    while s < CHUNK:
        level_masks.append(((sr // (2 * s)) == (sc // (2 * s))) & (((sr // s) % 2) == 1) & (((sc // s) % 2) == 0))
        s *= 2

    pairs = range(RW_PAIRS)
    sls = [slice(p * RW_LANES, (p + 1) * RW_LANES) for p in pairs]
    kk_ss = [jnp.dot((kk_raw[:, sl] * kk_raw[:, sl]).astype(BF16), head_ones, preferred_element_type=F32)
             for sl in sls]
    kk = [kk_raw[:, sl] / jnp.maximum(jnp.sqrt(ss), 1e-12) for sl, ss in zip(sls, kk_ss)]
    kka = [kk[p] * a[:, sls[p]] for p in pairs]
    for p in pairs:
        bend_ref[:, sls[p]] = (kka[p] * e_end[:, sls[p]]).astype(bend_ref.dtype)
    hsum = [jnp.dot(rkr[:, sl].astype(BF16), head_ones, preferred_element_type=F32) for sl in sls]
    for p in pairs:
        bonus_ref[:, sls[p]] = (hsum[p] * v[:, sls[p]]).astype(bonus_ref.dtype)

    a_s = [_stack_heads(-kk[p] * e_prev[:, sls[p]], head0).astype(BF16) for p in pairs]
    b_s = [_stack_heads(kka[p] * e_neg[:, sls[p]], head0).astype(BF16) for p in pairs]
    k_s = [_stack_heads(k_t[:, sl], head0).astype(BF16) for sl in sls]
    r_s = [_stack_heads(r[:, sl] * e_pos[:, sl], head0).astype(BF16) for sl in sls]
    v_s = [_stack_heads(v[:, sl], head0).astype(BF16) for sl in sls]

    l_ab = [jnp.where(strict, _dot_nt(a_s[p], b_s[p]), 0.0) for p in pairs]
    l_ak = [jnp.where(strict, _dot_nt(a_s[p], k_s[p]), 0.0).astype(BF16) for p in pairs]
    l_rk = [jnp.where(incl, _dot_nt(r_s[p], k_s[p]), 0.0).astype(BF16) for p in pairs]
    l_rb = [jnp.where(incl, _dot_nt(r_s[p], b_s[p]), 0.0) for p in pairs]
    for p in pairs:
        lrb_ref[:, sls[p]] = _unstack_heads(l_rb[p]).astype(lrb_ref.dtype)
    y0 = [_dot(l_rk[p], v_s[p]) for p in pairs]
    for p in pairs:
        y0_ref[:, sls[p]] = _unstack_heads(y0[p])
    x0 = [_dot(l_ak[p], v_s[p]) for p in pairs]

    t_inv = [eye + jnp.where(level_masks[0], l_ab[p], 0.0) for p in pairs]
    for m in level_masks[1:]:
        lt = [_dot(jnp.where(m, l_ab[p], 0.0), t_inv[p]) for p in pairs]
        t_inv = [t_inv[p] + _dot(t_inv[p], lt[p]) for p in pairs]

    w = [_dot(t_inv[p], a_s[p]) for p in pairs]
    u0 = [_dot(t_inv[p], x0[p]) for p in pairs]
    for p in pairs:
        w_ref[:, sls[p]] = _unstack_heads(w[p]).astype(w_ref.dtype)
        u0_ref[:, sls[p]] = _unstack_heads(u0[p])


def _rwkv_scan_body(w_ref, u0_ref, rt_ref, y0_ref, lrb_ref, kend_ref, bend_ref, v_ref, gam_ref,
                    g_ref, bonus_ref, lnw_ref, lnb_ref, o_ref, s_ref):
    @pl.when(pl.program_id(1) == 0)
    def _():
        s_ref[...] = jnp.zeros_like(s_ref)

    same_head, head0 = _pair_masks()
    head_avg = (same_head.astype(F32) * (1.0 / A_HEAD)).astype(BF16)
    gam = gam_ref[0]

    pairs = range(RW_PAIRS)
    sls = [slice(p * RW_LANES, (p + 1) * RW_LANES) for p in pairs]
    s = [s_ref[p] for p in pairs]
    ws = [lax.dot_general(jnp.concatenate([w_ref[:, sl], rt_ref[:, sl]], axis=0), s[p].astype(BF16),
                          (((1,), (1,)), ((), ())), preferred_element_type=F32)
          for p, sl in zip(pairs, sls)]
    u = [ws[p][0:CHUNK, :] + u0_ref[:, sls[p]] for p in pairs]
    s_add = [_dot_tn(jnp.concatenate([v_ref[:, sl], u[p].astype(BF16)], axis=0),
                     jnp.concatenate([kend_ref[:, sl], bend_ref[:, sl]], axis=0))
             for p, sl in zip(pairs, sls)]
    for p in pairs:
        s_ref[p] = s[p] * gam[:, sls[p]] + jnp.where(same_head, s_add[p], 0.0)
    y = [ws[p][CHUNK:, :] + y0_ref[:, sls[p]]
         + jnp.dot(lrb_ref[:, sls[p]], _stack_heads(u[p], head0).astype(BF16), preferred_element_type=F32)
         for p in pairs]

    y_hl = [_split_bf16(y[p], 2) for p in pairs]
    mu = [jnp.dot(h, head_avg, preferred_element_type=F32) + jnp.dot(l, head_avg, preferred_element_type=F32)
          for h, l in y_hl]
    d = [y[p] - mu[p] for p in pairs]
    var = [jnp.dot((d[p] * d[p]).astype(BF16), head_avg, preferred_element_type=F32) for p in pairs]
    for p, sl in zip(pairs, sls):
        yn = d[p] * lax.rsqrt(var[p] + GN_EPS) * lnw_ref[:, sl] + lnb_ref[:, sl]
        o_ref[:, sl] = ((yn + bonus_ref[:, sl]) * g_ref[:, sl].astype(F32)).astype(o_ref.dtype)


def _rwkv_mixer(z, p, bsz, seq):
    nchunk = seq // CHUNK
    t = bsz * seq
    rblk = Z_RKV // A_DIM
    lblk = Z_LORA // LORA_IN
    halo_per_chunk = CHUNK // RW_HALO

    def zspec(width, blk):
        return pl.BlockSpec((CHUNK, width), lambda b, c: (b * nchunk + c, blk))

    def hspec(width, blk):
        return pl.BlockSpec((RW_HALO, width),
                            lambda b, c: (jnp.maximum((b * nchunk + c) * halo_per_chunk - 1, 0), blk))

    full = lambda shape: pl.BlockSpec(shape, lambda b, c: (0, 0))
    tok = pl.BlockSpec((CHUNK, A_DIM), lambda b, c: (b * nchunk + c, 0))
    gam_spec = pl.BlockSpec((1, 1, A_DIM), lambda b, c: (b * nchunk + c, 0, 0))
    bf = jax.ShapeDtypeStruct((t, A_DIM), BF16)
    f32 = jax.ShapeDtypeStruct((t, A_DIM), F32)
    gam_shape = jax.ShapeDtypeStruct((bsz * nchunk, 1, A_DIM), F32)

    prep = pl.pallas_call(
        _rwkv_prep_body,
        out_shape=(bf, f32, bf, f32, bf, bf, bf, bf, gam_shape, bf, f32),
        grid=(bsz, nchunk),
        in_specs=[zspec(A_DIM, rblk), zspec(A_DIM, rblk + 1), zspec(A_DIM, rblk + 2), zspec(LORA_IN, lblk),
                  hspec(A_DIM, rblk), hspec(A_DIM, rblk + 1), hspec(A_DIM, rblk + 2), hspec(LORA_IN, lblk),
                  full((1, A_DIM)), full((1, A_DIM)), full((1, A_DIM)), full((1, LORA_IN)),
                  full((1, A_DIM)), full((LORA_PAD, A_DIM)), full((LORA_PAD, A_DIM)),
                  full((1, A_DIM)), full((LORA_PAD, A_DIM)), full((LORA_G, A_DIM)),
                  full((1, A_DIM)), full((1, A_DIM)), full((1, A_DIM))],
        out_specs=(tok, tok, tok, tok, tok, tok, tok, tok, gam_spec, tok, tok),
        compiler_params=_cparams(("parallel", "parallel")),
        name="rwkv7_prep",
    )(z, z, z, z, z, z, z, z,
      p["mix_r"], p["mix_k"], p["mix_v"], p["mix_l"],
      p["w0"], p["w2h"], p["w2l"], p["a0"], p["a2"], p["g2"], p["k_k"], p["k_a"], p["r_k"])

    return pl.pallas_call(
        _rwkv_scan_body,
        out_shape=bf,
        grid=(bsz, nchunk),
        in_specs=[tok, tok, tok, tok, tok, tok, tok, tok, gam_spec, tok, tok,
                  full((1, A_DIM)), full((1, A_DIM))],
        out_specs=tok,
        scratch_shapes=[pltpu.VMEM((RW_PAIRS, RW_LANES, RW_LANES), F32)],
        compiler_params=_cparams(("parallel", "arbitrary")),
        name="rwkv7_scan",
    )(*prep, p["ln_w"], p["ln_b"])


def _rope_body(zq_ref, zk_ref, zv_ref, cos_ref, sin_ref, q1_ref, q2_ref, k_ref, v_ref):
    reps = B_DIM // 128
    cos = jnp.concatenate([cos_ref[...]] * reps, axis=1)
    sin = jnp.concatenate([sin_ref[...]] * reps, axis=1)
    lane = lax.broadcasted_iota(jnp.int32, cos.shape, 1)
    first_half = (lane % B_QK) < (B_QK // 2)
    comp0 = (lane % B_V) < B_QK

    def rope(x):
        partner = jnp.where(first_half, pltpu.roll(x, B_DIM - B_QK // 2, 1), pltpu.roll(x, B_QK // 2, 1))
        return x * cos + partner * sin

    q = rope(zq_ref[...]) * (B_QK ** -0.5)
    q1_ref[...] = jnp.where(comp0, q, 0.0).astype(BF16)
    q2_ref[...] = jnp.where(comp0, 0.0, q).astype(BF16)
    k_ref[...] = rope(zk_ref[...]).astype(BF16)
    v_ref[...] = zv_ref[...].astype(BF16)


def _rope_prep(z, cos_t, sin_t, tm):
    t = z.shape[0]
    qblk = Z_QKV // B_DIM
    out = jax.ShapeDtypeStruct((t, B_DIM), BF16)
    ospec = pl.BlockSpec((tm, B_DIM), lambda i: (i, 0))
    return pl.pallas_call(
        _rope_body,
        out_shape=(out, out, out, out),
        grid=(t // tm,),
        in_specs=[pl.BlockSpec((tm, B_DIM), lambda i: (i, qblk)),
                  pl.BlockSpec((tm, B_DIM), lambda i: (i, qblk + 1)),
                  pl.BlockSpec((tm, B_DIM), lambda i: (i, qblk + 2)),
                  pl.BlockSpec((tm, 128), lambda i: (i, 0)),
                  pl.BlockSpec((tm, 128), lambda i: (i, 0))],
        out_specs=(ospec, ospec, ospec, ospec),
        compiler_params=_cparams(("parallel",)),
        name="rope_prep",
    )(z, z, z, cos_t, sin_t)


def _attn_body(q1_ref, q2_ref, k_ref, v_ref, lam_ref, sub_ref, o_ref,
               m1_ref, a1_ref, m2_ref, a2_ref, *, tq):
    i = pl.program_id(2)
    nlane = tq // 128

    m1_ref[...] = jnp.full_like(m1_ref, NEG_BIG)
    m2_ref[...] = jnp.full_like(m2_ref, NEG_BIG)
    a1_ref[...] = jnp.zeros_like(a1_ref)
    a2_ref[...] = jnp.zeros_like(a2_ref)
    ones = jnp.ones((tq, B_V), BF16)

    def update(j, masked):
        rows = pl.ds(pl.multiple_of(j * tq, tq), tq)
        kb = k_ref[0, rows, :]
        vext = jnp.concatenate([v_ref[0, rows, :], ones], axis=1)
        if masked:
            row = lax.broadcasted_iota(jnp.int32, (tq, tq), 0)
            col = lax.broadcasted_iota(jnp.int32, (tq, tq), 1)
            visible = (col // CHUNK) <= (row // CHUNK)
        for q_ref, m_ref, a_ref in ((q1_ref, m1_ref, a1_ref), (q2_ref, m2_ref, a2_ref)):
            s = lax.dot_general(q_ref[0], kb, (((1,), (1,)), ((), ())), preferred_element_type=F32)
            if masked:
                s = jnp.where(visible, s, NEG_BIG)
            cols = [s[:, c * 128:(c + 1) * 128] for c in range(nlane)]
            mx = cols[0]
            for c in cols[1:]:
                mx = jnp.maximum(mx, c)
            m_old = m_ref[...]
            m_new = jnp.maximum(m_old, jnp.max(mx, axis=-1, keepdims=True))
            alpha = jnp.exp(m_old - m_new)
            p = jnp.concatenate([jnp.exp((c - m_new).astype(BF16)) for c in cols], axis=1)
            pv = jnp.dot(p, vext, preferred_element_type=F32)
            a_ref[...] = jnp.concatenate([alpha, alpha], axis=1) * a_ref[...] + pv
            m_ref[...] = m_new

    def body(j, carry):
        update(j, False)
        return carry

    lax.fori_loop(0, i, body, 0)
    update(i, True)

    a1 = a1_ref[...]
    a2 = a2_ref[...]
    o = a1[:, :B_V] / a1[:, B_V:] - lam_ref[...] * (a2[:, :B_V] / a2[:, B_V:])
    o = o * lax.rsqrt(jnp.mean(o * o, axis=-1, keepdims=True) + EPS) * sub_ref[...]
    o_ref[0] = o.astype(o_ref.dtype)


def _diff_attention(q1, q2, k, v, lam, sub, tq):
    bsz, seq, _ = q1.shape
    nq = seq // tq
    qspec = pl.BlockSpec((1, tq, B_V), lambda b, h, i: (b, i, h))
    kspec = pl.BlockSpec((1, seq, B_V), lambda b, h, i: (b, 0, h))
    pspec = pl.BlockSpec((1, B_V), lambda b, h, i: (0, 0))
    return pl.pallas_call(
        functools.partial(_attn_body, tq=tq),
        out_shape=jax.ShapeDtypeStruct((bsz, seq, B_DIM), BF16),
        grid=(bsz, B_HEADS, nq),
        in_specs=[qspec, qspec, kspec, kspec, pspec, pspec],
        out_specs=pl.BlockSpec((1, tq, B_V), lambda b, h, i: (b, i, h)),
        scratch_shapes=[pltpu.VMEM((tq, 128), F32), pltpu.VMEM((tq, 2 * B_V), F32),
                        pltpu.VMEM((tq, 128), F32), pltpu.VMEM((tq, 2 * B_V), F32)],
        compiler_params=_cparams(("parallel", "parallel", "arbitrary")),
        name="diff_attention",
    )(q1, q2, k, v, lam, sub)


S5_SUB = 8


def _s5_body(u_ref, bd_ref, cd_ref, pw_ref, d_ref, gw_ref, gb_ref, o_ref, xs_ref, carry_ref, *, lt):
    t = pl.program_id(1)

    @pl.when(t == 0)
    def _():
        carry_ref[...] = jnp.zeros_like(carry_ref)

    u = u_ref[...]
    bu = _dot(u, bd_ref[...])
    re = bu[:, :C_NS]
    im = bu[:, C_NS:]

    row = lax.broadcasted_iota(jnp.int32, (lt, C_NS), 0) % S5_SUB
    for d in (1, 2, 4):
        pr = pw_ref[d - 1:d, :C_NS]
        pi = pw_ref[d - 1:d, C_NS:]
        keep = row >= d
        sre = jnp.where(keep, pltpu.roll(re, d, 0), 0.0)
        sim = jnp.where(keep, pltpu.roll(im, d, 0), 0.0)
        re, im = re + pr * sre - pi * sim, im + pr * sim + pi * sre
    xs_ref[:, :C_NS] = re
    xs_ref[:, C_NS:] = im

    pw_re = pw_ref[:, :C_NS]
    pw_im = pw_ref[:, C_NS:]

    def group(gidx, carry):
        cr, ci = carry
        rows = pl.ds(pl.multiple_of(gidx * S5_SUB, S5_SUB), S5_SUB)
        nre = xs_ref[rows, :C_NS] + pw_re * cr - pw_im * ci
        nim = xs_ref[rows, C_NS:] + pw_re * ci + pw_im * cr
        xs_ref[rows, :C_NS] = nre
        xs_ref[rows, C_NS:] = nim
        return nre[S5_SUB - 1:S5_SUB, :], nim[S5_SUB - 1:S5_SUB, :]

    cr, ci = lax.fori_loop(0, lt // S5_SUB, group, (carry_ref[0:1, :C_NS], carry_ref[0:1, C_NS:]))
    carry_ref[0:1, :C_NS] = cr
    carry_ref[0:1, C_NS:] = ci

    y = _dot(xs_ref[...], cd_ref[...]) + d_ref[...] * u
    y = _gelu_tanh(y)
    y = y * jax.nn.sigmoid(_dot(y, gw_ref[...]) + gb_ref[...])
    o_ref[...] = y.astype(o_ref.dtype)


def _s5_mixer(z, p, bsz, seq, lt):
    nt = seq // lt
    ublk = Z_U // C_DIM
    full = lambda shape: pl.BlockSpec(shape, lambda b, t: (0, 0))
    return pl.pallas_call(
        functools.partial(_s5_body, lt=lt),
        out_shape=jax.ShapeDtypeStruct((bsz * seq, C_DIM), BF16),
        grid=(bsz, nt),
        in_specs=[pl.BlockSpec((lt, C_DIM), lambda b, t: (b * nt + t, ublk)),
                  full((C_DIM, 2 * C_NS)), full((2 * C_NS, C_DIM)), full((S5_SUB, 2 * C_NS)),
                  full((1, C_DIM)), full((C_DIM, C_DIM)), full((1, C_DIM))],
        out_specs=pl.BlockSpec((lt, C_DIM), lambda b, t: (b * nt + t, 0)),
        scratch_shapes=[pltpu.VMEM((lt, 2 * C_NS), F32), pltpu.VMEM((S5_SUB, 2 * C_NS), F32)],
        compiler_params=_cparams(("parallel", "arbitrary")),
        name="s5_mixer",
    )(z, p["bd"], p["cd"], p["pw"], p["d"], p["glu_w"], p["glu_b"])


def _merge_body(ya_ref, yb_ref, yc_ref, g0_ref, g1_ref, g2_ref, bg_ref, pa_ref, pb_ref, pc_ref, o_ref):
    bg = bg_ref[...]
    m = jax.nn.sigmoid(g0_ref[...] + bg[:, 0:D_MODEL]) * jnp.dot(
        ya_ref[...], pa_ref[...], preferred_element_type=F32)
    m = m + jax.nn.sigmoid(g1_ref[...] + bg[:, D_MODEL:2 * D_MODEL]) * jnp.dot(
        yb_ref[...], pb_ref[...], preferred_element_type=F32)
    m = m + jax.nn.sigmoid(g2_ref[...] + bg[:, 2 * D_MODEL:]) * jnp.dot(
        yc_ref[...], pc_ref[...], preferred_element_type=F32)
    o_ref[...] = m.astype(o_ref.dtype)


def _merge(ya, yb, yc, z, bg, pa, pb, pc, tm):
    t = ya.shape[0]
    full = lambda shape: pl.BlockSpec(shape, lambda i: (0, 0))
    return pl.pallas_call(
        _merge_body,
        out_shape=jax.ShapeDtypeStruct((t, D_MODEL), BF16),
        grid=(t // tm,),
        in_specs=[pl.BlockSpec((tm, A_DIM), lambda i: (i, 0)),
                  pl.BlockSpec((tm, B_DIM), lambda i: (i, 0)),
                  pl.BlockSpec((tm, C_DIM), lambda i: (i, 0)),
                  pl.BlockSpec((tm, D_MODEL), lambda i: (i, 0)),
                  pl.BlockSpec((tm, D_MODEL), lambda i: (i, 1)),
                  pl.BlockSpec((tm, D_MODEL), lambda i: (i, 2)),
                  full((1, N_BRANCH * D_MODEL)),
                  full((A_DIM, D_MODEL)), full((B_DIM, D_MODEL)), full((C_DIM, D_MODEL))],
        out_specs=pl.BlockSpec((tm, D_MODEL), lambda i: (i, 0)),
        compiler_params=_cparams(("parallel",)),
        name="gated_merge",
    )(ya, yb, yc, z, z, z, bg, pa, pb, pc)


def _matmul_res_body(a_ref, w_ref, r_ref, o_ref):
    o_ref[...] = r_ref[...] + jnp.dot(a_ref[...], w_ref[...], preferred_element_type=F32)


def _matmul_res(a, w, res, tm):
    t, kdim = a.shape
    n = w.shape[1]
    return pl.pallas_call(
        _matmul_res_body,
        out_shape=jax.ShapeDtypeStruct((t, n), F32),
        grid=(t // tm,),
        in_specs=[pl.BlockSpec((tm, kdim), lambda i: (i, 0)),
                  pl.BlockSpec((kdim, n), lambda i: (0, 0)),
                  pl.BlockSpec((tm, n), lambda i: (i, 0))],
        out_specs=pl.BlockSpec((tm, n), lambda i: (i, 0)),
        compiler_params=_cparams(("parallel",)),
        name="out_proj_residual",
    )(a, w, res)


FFN_HALO = 8


def _ffn_body(x_ref, xh_ref, g_ref, wv_ref, wg_ref, cw_ref, wd_ref, o_ref, h_ref, hh_ref, acc_ref,
              *, tm, blocks_per_seq):
    i = pl.program_id(0)
    j = pl.program_id(1)

    def norm(x):
        ms = jnp.mean(x * x, axis=-1, keepdims=True)
        return x * lax.rsqrt(ms + EPS) * g_ref[...]

    @pl.when(j == 0)
    def _():
        h_ref[...] = norm(x_ref[...]).astype(BF16)
        seq_start = (i % blocks_per_seq) == 0
        hh_ref[...] = jnp.where(seq_start, 0.0, norm(xh_ref[...])).astype(BF16)
        acc_ref[...] = jnp.zeros_like(acc_ref)

    h = h_ref[...]
    val = jnp.dot(h, wv_ref[...], preferred_element_type=F32)
    gate = jnp.dot(h, wg_ref[...], preferred_element_type=F32)
    gate_halo = jnp.dot(hh_ref[...], wg_ref[...], preferred_element_type=F32)

    row = lax.broadcasted_iota(jnp.int32, gate.shape, 0)
    gm1 = jnp.where(row == 0, gate_halo[FFN_HALO - 1:FFN_HALO, :], pltpu.roll(gate, 1, 0))
    gm2 = jnp.where(row == 0, gate_halo[FFN_HALO - 2:FFN_HALO - 1, :],
                    jnp.where(row == 1, gate_halo[FFN_HALO - 1:FFN_HALO, :], pltpu.roll(gate, 2, 0)))
    cw = cw_ref[...]
    conv = cw[0:1, :] * gm2 + cw[1:2, :] * gm1 + cw[2:3, :] * gate
    act = (_gelu_tanh(conv) * val).astype(BF16)
    acc_ref[...] += jnp.dot(act, wd_ref[...], preferred_element_type=F32)

    @pl.when(j == pl.num_programs(1) - 1)
    def _():
        o_ref[...] = x_ref[...] + acc_ref[...]


def _ffn(x, g, w_up, conv_w, w_down, seq, tm, tn):
    t, d = x.shape
    nff = D_FF // tn
    halo_per_block = tm // FFN_HALO
    return pl.pallas_call(
        functools.partial(_ffn_body, tm=tm, blocks_per_seq=seq // tm),
        out_shape=jax.ShapeDtypeStruct((t, d), F32),
        grid=(t // tm, nff),
        in_specs=[pl.BlockSpec((tm, d), lambda i, j: (i, 0)),
                  pl.BlockSpec((FFN_HALO, d), lambda i, j: (jnp.maximum(i * halo_per_block - 1, 0), 0)),
                  pl.BlockSpec((1, d), lambda i, j: (0, 0)),
                  pl.BlockSpec((d, tn), lambda i, j: (0, j)),
                  pl.BlockSpec((d, tn), lambda i, j: (0, nff + j)),
                  pl.BlockSpec((3, tn), lambda i, j: (0, j)),
                  pl.BlockSpec((tn, d), lambda i, j: (j, 0))],
        out_specs=pl.BlockSpec((tm, d), lambda i, j: (i, 0)),
        scratch_shapes=[pltpu.VMEM((tm, d), BF16), pltpu.VMEM((FFN_HALO, d), BF16),
                        pltpu.VMEM((tm, d), F32)],
        compiler_params=_cparams(("parallel", "arbitrary")),
        name="conv_glu_ffn",
    )(x, x, g, w_up, w_up, conv_w, w_down)


RW_IN = 3 * A_DIM + LORA_W + LORA_A + LORA_G
B_IN = 3 * B_DIM


def _pad_cols(w, width):
    return jnp.pad(w, ((0, 0), (0, width - w.shape[1])))


def _layout_w_in(w):
    o = 0
    rkv = w[:, o:o + 3 * A_DIM]; o += 3 * A_DIM
    zw = w[:, o:o + LORA_W]; o += LORA_W
    za = w[:, o:o + LORA_A]; o += LORA_A
    zg = w[:, o:o + LORA_G]; o += LORA_G
    qkv = w[:, o:o + B_IN]; o += B_IN
    u = w[:, o:o + C_DIM]; o += C_DIM
    gates = w[:, o:]
    out = jnp.concatenate([gates, rkv, qkv, _pad_cols(zw, LORA_PAD), _pad_cols(za, LORA_PAD), zg, u], axis=1)
    return out.astype(BF16)


def _rwkv_params(mix, w0, w2, a0, a2, g2, k_k, k_a, r_k, ln_w, ln_b):
    row = lambda v: v.reshape(1, -1)
    o = 3 * A_DIM
    mix_l = jnp.concatenate([
        jnp.pad(mix[o:o + LORA_W], (0, LORA_PAD - LORA_W)),
        jnp.pad(mix[o + LORA_W:o + LORA_W + LORA_A], (0, LORA_PAD - LORA_A)),
        mix[o + LORA_W + LORA_A:]])
    pad_rows = lambda m: jnp.pad(m, ((0, LORA_PAD - m.shape[0]), (0, 0)))
    w2 = pad_rows(w2)
    w2h = w2.astype(BF16)
    w2l = (w2 - w2h.astype(F32)).astype(BF16)
    return dict(mix_r=row(mix[0:A_DIM]), mix_k=row(mix[A_DIM:2 * A_DIM]), mix_v=row(mix[2 * A_DIM:3 * A_DIM]),
                mix_l=row(mix_l), w0=row(w0), w2h=w2h, w2l=w2l, a0=row(a0), a2=pad_rows(a2).astype(BF16),
                g2=g2.astype(BF16), k_k=row(k_k), k_a=row(k_a), r_k=row(r_k), ln_w=row(ln_w), ln_b=row(ln_b))


def _s5_params(lam_re, lam_im, log_dt, b_re, b_im, c_re, c_im, d, glu_w, glu_b):
    dt = jnp.exp(log_dt)[:, None]
    er = jnp.exp(lam_re * dt)
    ab_re = er * jnp.cos(lam_im * dt)
    ab_im = er * jnp.sin(lam_im * dt)
    den = lam_re * lam_re + lam_im * lam_im
    nr = ab_re - 1.0
    f_re = (nr * lam_re + ab_im * lam_im) / den
    f_im = (ab_im * lam_re - nr * lam_im) / den
    bb_re = f_re[..., None] * b_re - f_im[..., None] * b_im
    bb_im = f_re[..., None] * b_im + f_im[..., None] * b_re
    eye_g = jnp.eye(C_GROUPS, dtype=F32)
    bd_re = jnp.einsum("gpc,gh->gchp", bb_re, eye_g).reshape(C_DIM, C_NS)
    bd_im = jnp.einsum("gpc,gh->gchp", bb_im, eye_g).reshape(C_DIM, C_NS)
    cd_re = jnp.einsum("gcp,gh->gphc", c_re, eye_g).reshape(C_NS, C_DIM)
    cd_im = jnp.einsum("gcp,gh->gphc", c_im, eye_g).reshape(C_NS, C_DIM)
    pr, pi = [ab_re.reshape(-1)], [ab_im.reshape(-1)]
    for _ in range(S5_SUB - 1):
        pr.append(pr[-1] * pr[0] - pi[-1] * pi[0])
        pi.append(pr[-2] * pi[0] + pi[-1] * pr[0])
    pw = jnp.concatenate([jnp.stack(pr), jnp.stack(pi)], axis=1)
    return dict(bd=jnp.concatenate([bd_re, bd_im], axis=1).astype(BF16),
                cd=jnp.concatenate([cd_re, -cd_im], axis=0).astype(BF16),
                pw=pw, d=d.reshape(1, -1), glu_w=glu_w.astype(BF16), glu_b=glu_b.reshape(1, -1))


def kernel(x, positions, norm_mix, norm_ffn, w_in, b_gate, rw_mix, rw_w0, rw_w2, rw_a0, rw_a2, rw_g2, rw_k_k, rw_k_a, rw_r_k, rw_ln_w, rw_ln_b, da_lq1, da_lk1, da_lq2, da_lk2, da_subln, s5_lam_re, s5_lam_im, s5_log_dt, s5_b_re, s5_b_im, s5_c_re, s5_c_im, s5_d, s5_glu_w, s5_glu_b, proj_a, proj_b, proj_c, w_out, ffn_up, ffn_conv, ffn_down, norm_final):
    bsz, seq, d = x.shape
    t = bsz * seq
    tm_proj = min(1024, seq)
    tm_ffn = min(512, seq)
    tm_small = min(256, seq)
    tq = min(512, seq)
    lt = min(128, seq)

    inv_freq = ROPE_THETA ** (-jnp.arange(0, B_QK, 2, dtype=F32) / B_QK)
    ang = positions.astype(F32)[..., None] * inv_freq
    cos, sin = jnp.cos(ang), jnp.sin(ang)
    cos_t = jnp.concatenate([cos, cos, cos, cos], axis=-1).reshape(t, 128)
    sin_t = jnp.concatenate([-sin, sin, -sin, sin], axis=-1).reshape(t, 128)

    xf = x.reshape(t, d)
    for l in range(DEPTH):
        z = _norm_matmul(xf, norm_mix[l].reshape(1, d), _layout_w_in(w_in[l]), tm_proj, 512)

        rp = _rwkv_params(rw_mix[l], rw_w0[l], rw_w2[l], rw_a0[l], rw_a2[l], rw_g2[l], rw_k_k[l],
                          rw_k_a[l], rw_r_k[l].reshape(-1), rw_ln_w[l], rw_ln_b[l])
        y_a = _rwkv_mixer(z, rp, bsz, seq)

        lam_init = 0.8 - 0.6 * math.exp(-0.3 * l)
        lam = jnp.exp(jnp.sum(da_lq1[l] * da_lk1[l])) - jnp.exp(jnp.sum(da_lq2[l] * da_lk2[l])) + lam_init
        q1, q2, kr, vr = _rope_prep(z, cos_t, sin_t, tm_ffn)
        shp = (bsz, seq, B_DIM)
        y_b = _diff_attention(q1.reshape(shp), q2.reshape(shp), kr.reshape(shp), vr.reshape(shp),
                              jnp.full((1, B_V), lam, F32),
                              (da_subln[l] * (1.0 - lam_init)).reshape(1, B_V), tq).reshape(t, B_DIM)

        sp = _s5_params(s5_lam_re[l], s5_lam_im[l], s5_log_dt[l], s5_b_re[l], s5_b_im[l], s5_c_re[l],
                        s5_c_im[l], s5_d[l], s5_glu_w[l], s5_glu_b[l])
        y_c = _s5_mixer(z, sp, bsz, seq, lt)

        merged = _merge(y_a, y_b, y_c, z, b_gate[l].reshape(1, -1), proj_a[l].astype(BF16),
                        proj_b[l].astype(BF16), proj_c[l].astype(BF16), tm_small)
        xf = _matmul_res(merged, w_out[l].astype(BF16), xf, tm_ffn)
        xf = _ffn(xf, norm_ffn[l].reshape(1, d), ffn_up[l].astype(BF16), ffn_conv[l],
                  ffn_down[l].astype(BF16), seq, tm_ffn, 512)
    return _final_norm(xf, norm_final.reshape(1, d), tm_ffn).reshape(bsz, seq, d)
```

```python
import functools
import math

import jax
import jax.numpy as jnp
import numpy as np
from jax import lax
from jax.experimental import pallas as pl
from jax.experimental.pallas import tpu as pltpu

F32 = jnp.float32
BF16 = jnp.bfloat16

D_MODEL = 2048
DEPTH = 4
CHUNK = 64
EPS = 1e-6
ROPE_THETA = 10000.0

A_HEAD = 64
A_DIM = 768
A_HEADS = 12
LORA_W = 96
LORA_A = 96
LORA_G = 256
LORA_PAD = 128
GN_EPS = 64e-5

B_QK = 64
B_V = 128
B_DIM = 768
B_HEADS = 6

C_GROUP = 16
C_DIM = 512
C_GROUPS = 32
C_STATE = 64
C_NS = C_GROUPS * C_STATE

N_BRANCH = 3
D_FF = 5632

Z_GATE = 0
Z_RKV = N_BRANCH * D_MODEL
Z_QKV = Z_RKV + 3 * A_DIM
Z_LORA = Z_QKV + 3 * B_DIM
LORA_IN = 2 * LORA_PAD + LORA_G
Z_U = Z_LORA + LORA_IN
Z_COLS = Z_U + C_DIM

VMEM_LIMIT = 56 * 1024 * 1024
NEG_BIG = -1e30


def _cparams(sem):
    return pltpu.CompilerParams(dimension_semantics=sem, vmem_limit_bytes=VMEM_LIMIT)


def _gelu_tanh(x):
    return 0.5 * x * (1.0 + jnp.tanh(math.sqrt(2.0 / math.pi) * (x + 0.044715 * (x * x * x))))


def _dot(a, b):
    return jnp.dot(a.astype(BF16), b.astype(BF16), preferred_element_type=F32)


def _dot_f32(a, b):
    return jnp.dot(a, b, preferred_element_type=F32, precision=lax.Precision.HIGHEST)


def _dot_nt(a, b):
    return lax.dot_general(a.astype(BF16), b.astype(BF16), (((1,), (1,)), ((), ())),
                           preferred_element_type=F32)


def _dot_tn(a, b, precision=None):
    return lax.dot_general(a, b, (((0,), (0,)), ((), ())), preferred_element_type=F32,
                           precision=precision)


def _norm_matmul_body(x_ref, g_ref, w_ref, o_ref, h_ref):
    @pl.when(pl.program_id(1) == 0)
    def _():
        x = x_ref[...]
        ms = jnp.mean(x * x, axis=-1, keepdims=True)
        h_ref[...] = (x * lax.rsqrt(ms + EPS) * g_ref[...]).astype(BF16)

    o_ref[...] = jnp.dot(h_ref[...], w_ref[...], preferred_element_type=F32).astype(o_ref.dtype)


def _norm_matmul(x, g, w, tm, tn):
    t, d = x.shape
    n = w.shape[1]
    return pl.pallas_call(
        _norm_matmul_body,
        out_shape=jax.ShapeDtypeStruct((t, n), BF16),
        grid=(t // tm, n // tn),
        in_specs=[pl.BlockSpec((tm, d), lambda i, j: (i, 0)),
                  pl.BlockSpec((1, d), lambda i, j: (0, 0)),
                  pl.BlockSpec((d, tn), lambda i, j: (0, j))],
        out_specs=pl.BlockSpec((tm, tn), lambda i, j: (i, j)),
        scratch_shapes=[pltpu.VMEM((tm, d), BF16)],
        compiler_params=_cparams(("parallel", "arbitrary")),
        name="norm_in_proj",
    )(x, g, w)


def _final_norm_body(x_ref, g_ref, o_ref):
    x = x_ref[...]
    ms = jnp.mean(x * x, axis=-1, keepdims=True)
    o_ref[...] = x * lax.rsqrt(ms + EPS) * g_ref[...]


def _final_norm(x, g, tm):
    t, d = x.shape
    return pl.pallas_call(
        _final_norm_body,
        out_shape=jax.ShapeDtypeStruct((t, d), F32),
        grid=(t // tm,),
        in_specs=[pl.BlockSpec((tm, d), lambda i: (i, 0)),
                  pl.BlockSpec((1, d), lambda i: (0, 0))],
        out_specs=pl.BlockSpec((tm, d), lambda i: (i, 0)),
        compiler_params=_cparams(("parallel",)),
        name="final_norm",
    )(x, g)


RW_LANES = 2 * A_HEAD
RW_ROWS = 2 * CHUNK
RW_PAIRS = A_HEADS // 2
RW_NC = 4
RW_HALO = 16


def _shift_rows(x, prev_row):
    row = lax.broadcasted_iota(jnp.int32, x.shape, 0)
    return jnp.where(row == 0, prev_row, pltpu.roll(x, 1, 0))


def _split_bf16(x, pieces):
    out = []
    for _ in range(pieces - 1):
        h = x.astype(BF16)
        out.append(h)
        x = x - h.astype(F32)
    out.append(x.astype(BF16))
    return out


def _pair_masks():
    lane_r = lax.broadcasted_iota(jnp.int32, (RW_LANES, RW_LANES), 0)
    lane_c = lax.broadcasted_iota(jnp.int32, (RW_LANES, RW_LANES), 1)
    same_head = (lane_r // A_HEAD) == (lane_c // A_HEAD)
    head0 = lax.broadcasted_iota(jnp.int32, (CHUNK, RW_LANES), 1) < A_HEAD
    return same_head, head0


def _stack_heads(x, head0):
    return jnp.concatenate([jnp.where(head0, x, 0.0), jnp.where(head0, 0.0, x)], axis=0)


def _unstack_heads(x):
    return x[0:CHUNK, :] + x[CHUNK:, :]


def _rwkv_prep_body(zr_ref, zk_ref, zv_ref, zl_ref, hr_ref, hk_ref, hv_ref, hl_ref,
                    mr_ref, mk_ref, mv_ref, ml_ref,
                    w0_ref, w2h_ref, w2l_ref, a0_ref, a2_ref, g2_ref, kk_ref, ka_ref, rk_ref,
                    w_ref, u0_ref, rt_ref, y0_ref, lrb_ref, kend_ref, bend_ref, v_ref, gam_ref,
                    g_ref, bonus_ref):
    first = pl.program_id(1) == 0

    def token_shift(z_ref, halo_ref, mix_ref):
        z = z_ref[...].astype(F32)
        prev = jnp.where(first, 0.0, halo_ref[RW_HALO - 1:RW_HALO, :].astype(F32))
        return z + (_shift_rows(z, prev) - z) * mix_ref[...]

    r = token_shift(zr_ref, hr_ref, mr_ref)
    k = token_shift(zk_ref, hk_ref, mk_ref)
    v = token_shift(zv_ref, hv_ref, mv_ref)
    zl = token_shift(zl_ref, hl_ref, ml_ref)
    zw = zl[:, 0:LORA_PAD]
    za = zl[:, LORA_PAD:2 * LORA_PAD]
    zg = zl[:, 2 * LORA_PAD:]

    th_h, th_l = _split_bf16(jnp.tanh(zw), 2)
    w2h = w2h_ref[...]
    pre_w = (w0_ref[...] + jnp.dot(th_h, w2h, preferred_element_type=F32)
             + jnp.dot(th_h, w2l_ref[...], preferred_element_type=F32)
             + jnp.dot(th_l, w2h, preferred_element_type=F32))
    neg = -pre_w
    softplus = jnp.maximum(neg, 0.0) + jnp.log(1.0 + jnp.exp(-jnp.abs(neg)))
    logw = -jnp.exp(-softplus - 0.5)
    a = jax.nn.sigmoid(a0_ref[...] + _dot(za, a2_ref[...]))
    g = _dot(jax.nn.sigmoid(zg), g2_ref[...])
    g_ref[...] = g.astype(g_ref.dtype)
    v_ref[...] = v.astype(v_ref.dtype)

    rows = RW_NC * CHUNK
    ti = lax.broadcasted_iota(jnp.int32, (rows, rows), 0)
    tj = lax.broadcasted_iota(jnp.int32, (rows, rows), 1)
    tri = ((tj <= ti) & ((ti // CHUNK) == (tj // CHUNK))).astype(BF16)
    lg = sum(jnp.dot(tri, piece, preferred_element_type=F32) for piece in _split_bf16(logw, 3))
    rss = [slice(ci * CHUNK, (ci + 1) * CHUNK) for ci in range(RW_NC)]
    lg_last = [lg[rs.stop - 1:rs.stop, :] for rs in rss]
    for ci in range(RW_NC):
        gam_ref[ci] = jnp.exp(lg_last[ci])

    kk_raw = k * kk_ref[...]
    k = k * (1.0 + (a - 1.0) * ka_ref[...])
    e_pos = jnp.exp(lg)
    e_neg = jnp.exp(-lg)
    e_prev = jnp.exp(lg - logw)
    e_end = jnp.concatenate([jnp.exp(last - lg[rs, :]) for last, rs in zip(lg_last, rss)], axis=0)
    rt_ref[...] = (r * e_pos).astype(rt_ref.dtype)
    kend_ref[...] = (k * e_end).astype(kend_ref.dtype)
    k_t = k * e_neg
    rkr = r * k * rk_ref[...]

    same_head, head0 = _pair_masks()
    head_ones = same_head.astype(BF16)
    sr = lax.broadcasted_iota(jnp.int32, (RW_ROWS, RW_ROWS), 0)
    sc = lax.broadcasted_iota(jnp.int32, (RW_ROWS, RW_ROWS), 1)
    same_blk = (sr // CHUNK) == (sc // CHUNK)
    strict = same_blk & ((sr % CHUNK) > (sc % CHUNK))
    incl = same_blk & ((sr % CHUNK) >= (sc % CHUNK))
    eye = (sr == sc).astype(F32)
    level_masks = []
    s = 1
    while s < CHUNK:
        level_masks.append(((sr // (2 * s)) == (sc // (2 * s))) & (((sr // s) % 2) == 1) & (((sc // s) % 2) == 0))
        s *= 2

    at = [(rs, slice(p * RW_LANES, (p + 1) * RW_LANES)) for rs in rss for p in range(RW_PAIRS)]
    chains = range(len(at))
    kk_ss = [jnp.dot((kk_raw[ix] * kk_raw[ix]).astype(BF16), head_ones, preferred_element_type=F32) for ix in at]
    kk = [kk_raw[ix] / jnp.maximum(jnp.sqrt(ss), 1e-12) for ix, ss in zip(at, kk_ss)]
    kka = [kk[c] * a[at[c]] for c in chains]
    for c in chains:
        bend_ref[at[c]] = (kka[c] * e_end[at[c]]).astype(bend_ref.dtype)
    hsum = [jnp.dot(rkr[ix].astype(BF16), head_ones, preferred_element_type=F32) for ix in at]
    for c in chains:
        bonus_ref[at[c]] = (hsum[c] * v[at[c]]).astype(bonus_ref.dtype)

    a_s = [_stack_heads(-kk[c] * e_prev[at[c]], head0).astype(BF16) for c in chains]
    b_s = [_stack_heads(kka[c] * e_neg[at[c]], head0).astype(BF16) for c in chains]
    k_s = [_stack_heads(k_t[ix], head0).astype(BF16) for ix in at]
    r_s = [_stack_heads(r[ix] * e_pos[ix], head0).astype(BF16) for ix in at]
    v_s = [_stack_heads(v[ix], head0).astype(BF16) for ix in at]

    l_ab = [jnp.where(strict, _dot_nt(a_s[c], b_s[c]), 0.0) for c in chains]
    l_ak = [jnp.where(strict, _dot_nt(a_s[c], k_s[c]), 0.0).astype(BF16) for c in chains]
    l_rk = [jnp.where(incl, _dot_nt(r_s[c], k_s[c]), 0.0).astype(BF16) for c in chains]
    l_rb = [jnp.where(incl, _dot_nt(r_s[c], b_s[c]), 0.0) for c in chains]
    for c in chains:
        lrb_ref[at[c]] = _unstack_heads(l_rb[c]).astype(lrb_ref.dtype)
    y0 = [_dot(l_rk[c], v_s[c]) for c in chains]
    for c in chains:
        y0_ref[at[c]] = _unstack_heads(y0[c])
    x0 = [_dot(l_ak[c], v_s[c]) for c in chains]

    t_inv = [eye + jnp.where(level_masks[0], l_ab[c], 0.0) for c in chains]
    for m in level_masks[1:]:
        lt = [_dot(jnp.where(m, l_ab[c], 0.0), t_inv[c]) for c in chains]
        t_inv = [t_inv[c] + _dot(t_inv[c], lt[c]) for c in chains]

    w = [_dot(t_inv[c], a_s[c]) for c in chains]
    u0 = [_dot(t_inv[c], x0[c]) for c in chains]
    for c in chains:
        w_ref[at[c]] = _unstack_heads(w[c]).astype(w_ref.dtype)
        u0_ref[at[c]] = _unstack_heads(u0[c])


def _rwkv_scan_body(w_ref, u0_ref, rt_ref, y0_ref, lrb_ref, kend_ref, bend_ref, v_ref, gam_ref,
                    g_ref, bonus_ref, lnw_ref, lnb_ref, o_ref, s_ref):
    @pl.when(pl.program_id(1) == 0)
    def _():
        s_ref[...] = jnp.zeros_like(s_ref)

    same_head, head0 = _pair_masks()
    head_avg = (same_head.astype(F32) * (1.0 / A_HEAD)).astype(BF16)
    gam = gam_ref[0]

    pairs = range(RW_PAIRS)
    sls = [slice(p * RW_LANES, (p + 1) * RW_LANES) for p in pairs]
    s = [s_ref[p] for p in pairs]
    ws = [lax.dot_general(jnp.concatenate([w_ref[:, sl], rt_ref[:, sl]], axis=0), s[p].astype(BF16),
                          (((1,), (1,)), ((), ())), preferred_element_type=F32)
          for p, sl in zip(pairs, sls)]
    u = [ws[p][0:CHUNK, :] + u0_ref[:, sls[p]] for p in pairs]
    s_add = [_dot_tn(jnp.concatenate([v_ref[:, sl], u[p].astype(BF16)], axis=0),
                     jnp.concatenate([kend_ref[:, sl], bend_ref[:, sl]], axis=0))
             for p, sl in zip(pairs, sls)]
    for p in pairs:
        s_ref[p] = s[p] * gam[:, sls[p]] + jnp.where(same_head, s_add[p], 0.0)
    y = [ws[p][CHUNK:, :] + y0_ref[:, sls[p]]
         + jnp.dot(lrb_ref[:, sls[p]], _stack_heads(u[p], head0).astype(BF16), preferred_element_type=F32)
         for p in pairs]

    y_hl = [_split_bf16(y[p], 2) for p in pairs]
    mu = [jnp.dot(h, head_avg, preferred_element_type=F32) + jnp.dot(l, head_avg, preferred_element_type=F32)
          for h, l in y_hl]
    d = [y[p] - mu[p] for p in pairs]
    var = [jnp.dot((d[p] * d[p]).astype(BF16), head_avg, preferred_element_type=F32) for p in pairs]
    for p, sl in zip(pairs, sls):
        yn = d[p] * lax.rsqrt(var[p] + GN_EPS) * lnw_ref[:, sl] + lnb_ref[:, sl]
        o_ref[:, sl] = ((yn + bonus_ref[:, sl]) * g_ref[:, sl].astype(F32)).astype(o_ref.dtype)


def _rwkv_mixer(z, p, bsz, seq):
    nchunk = seq // CHUNK
    t = bsz * seq
    rblk = Z_RKV // A_DIM
    lblk = Z_LORA // LORA_IN
    prows = RW_NC * CHUNK
    nprep = seq // prows
    halo_per_step = prows // RW_HALO

    def zspec(width, blk):
        return pl.BlockSpec((prows, width), lambda b, c: (b * nprep + c, blk))

    def hspec(width, blk):
        return pl.BlockSpec((RW_HALO, width),
                            lambda b, c: (jnp.maximum((b * nprep + c) * halo_per_step - 1, 0), blk))

    full = lambda shape: pl.BlockSpec(shape, lambda b, c: (0, 0))
    ptok = pl.BlockSpec((prows, A_DIM), lambda b, c: (b * nprep + c, 0))
    pgam = pl.BlockSpec((RW_NC, 1, A_DIM), lambda b, c: (b * nprep + c, 0, 0))
    tok = pl.BlockSpec((CHUNK, A_DIM), lambda b, c: (b * nchunk + c, 0))
    gam_spec = pl.BlockSpec((1, 1, A_DIM), lambda b, c: (b * nchunk + c, 0, 0))
    bf = jax.ShapeDtypeStruct((t, A_DIM), BF16)
    f32 = jax.ShapeDtypeStruct((t, A_DIM), F32)
    gam_shape = jax.ShapeDtypeStruct((bsz * nchunk, 1, A_DIM), F32)

    prep = pl.pallas_call(
        _rwkv_prep_body,
        out_shape=(bf, f32, bf, f32, bf, bf, bf, bf, gam_shape, bf, f32),
        grid=(bsz, nprep),
        in_specs=[zspec(A_DIM, rblk), zspec(A_DIM, rblk + 1), zspec(A_DIM, rblk + 2), zspec(LORA_IN, lblk),
                  hspec(A_DIM, rblk), hspec(A_DIM, rblk + 1), hspec(A_DIM, rblk + 2), hspec(LORA_IN, lblk),
                  full((1, A_DIM)), full((1, A_DIM)), full((1, A_DIM)), full((1, LORA_IN)),
                  full((1, A_DIM)), full((LORA_PAD, A_DIM)), full((LORA_PAD, A_DIM)),
                  full((1, A_DIM)), full((LORA_PAD, A_DIM)), full((LORA_G, A_DIM)),
                  full((1, A_DIM)), full((1, A_DIM)), full((1, A_DIM))],
        out_specs=(ptok, ptok, ptok, ptok, ptok, ptok, ptok, ptok, pgam, ptok, ptok),
        compiler_params=_cparams(("parallel", "parallel")),
        name="rwkv7_prep",
    )(z, z, z, z, z, z, z, z,
      p["mix_r"], p["mix_k"], p["mix_v"], p["mix_l"],
      p["w0"], p["w2h"], p["w2l"], p["a0"], p["a2"], p["g2"], p["k_k"], p["k_a"], p["r_k"])

    return pl.pallas_call(
        _rwkv_scan_body,
        out_shape=bf,
        grid=(bsz, nchunk),
        in_specs=[tok, tok, tok, tok, tok, tok, tok, tok, gam_spec, tok, tok,
                  full((1, A_DIM)), full((1, A_DIM))],
        out_specs=tok,
        scratch_shapes=[pltpu.VMEM((RW_PAIRS, RW_LANES, RW_LANES), F32)],
        compiler_params=_cparams(("parallel", "arbitrary")),
        name="rwkv7_scan",
    )(*prep, p["ln_w"], p["ln_b"])


def _rope_body(zq_ref, zk_ref, zv_ref, cos_ref, sin_ref, q1_ref, q2_ref, k_ref, v_ref):
    reps = B_DIM // 128
    cos = jnp.concatenate([cos_ref[...]] * reps, axis=1)
    sin = jnp.concatenate([sin_ref[...]] * reps, axis=1)
    lane = lax.broadcasted_iota(jnp.int32, cos.shape, 1)
    first_half = (lane % B_QK) < (B_QK // 2)
    comp0 = (lane % B_V) < B_QK

    def rope(x):
        partner = jnp.where(first_half, pltpu.roll(x, B_DIM - B_QK // 2, 1), pltpu.roll(x, B_QK // 2, 1))
        return x * cos + partner * sin

    q = rope(zq_ref[...].astype(F32)) * (B_QK ** -0.5)
    q1_ref[...] = jnp.where(comp0, q, 0.0).astype(BF16)
    q2_ref[...] = jnp.where(comp0, 0.0, q).astype(BF16)
    k_ref[...] = rope(zk_ref[...].astype(F32)).astype(BF16)
    v_ref[...] = zv_ref[...]


def _rope_prep(z, cos_t, sin_t, tm):
    t = z.shape[0]
    qblk = Z_QKV // B_DIM
    out = jax.ShapeDtypeStruct((t, B_DIM), BF16)
    ospec = pl.BlockSpec((tm, B_DIM), lambda i: (i, 0))
    return pl.pallas_call(
        _rope_body,
        out_shape=(out, out, out, out),
        grid=(t // tm,),
        in_specs=[pl.BlockSpec((tm, B_DIM), lambda i: (i, qblk)),
                  pl.BlockSpec((tm, B_DIM), lambda i: (i, qblk + 1)),
                  pl.BlockSpec((tm, B_DIM), lambda i: (i, qblk + 2)),
                  pl.BlockSpec((tm, 128), lambda i: (i, 0)),
                  pl.BlockSpec((tm, 128), lambda i: (i, 0))],
        out_specs=(ospec, ospec, ospec, ospec),
        compiler_params=_cparams(("parallel",)),
        name="rope_prep",
    )(z, z, z, cos_t, sin_t)


def _attn_body(q1_ref, q2_ref, k_ref, v_ref, lam_ref, sub_ref, o_ref,
               m1_ref, a1_ref, m2_ref, a2_ref, *, tq):
    i = pl.program_id(2)
    nlane = tq // 128

    m1_ref[...] = jnp.full_like(m1_ref, NEG_BIG)
    m2_ref[...] = jnp.full_like(m2_ref, NEG_BIG)
    a1_ref[...] = jnp.zeros_like(a1_ref)
    a2_ref[...] = jnp.zeros_like(a2_ref)
    ones = jnp.ones((tq, B_V), BF16)

    def update(j, masked):
        rows = pl.ds(pl.multiple_of(j * tq, tq), tq)
        kb = k_ref[0, rows, :]
        vext = jnp.concatenate([v_ref[0, rows, :], ones], axis=1)
        if masked:
            row = lax.broadcasted_iota(jnp.int32, (tq, tq), 0)
            col = lax.broadcasted_iota(jnp.int32, (tq, tq), 1)
            visible = (col // CHUNK) <= (row // CHUNK)
        comps = ((q1_ref, m1_ref, a1_ref), (q2_ref, m2_ref, a2_ref))
        scores = [lax.dot_general(q_ref[0], kb, (((1,), (1,)), ((), ())), preferred_element_type=F32)
                  for q_ref, _, _ in comps]
        probs, alphas = [], []
        for s, (_, m_ref, _) in zip(scores, comps):
            if masked:
                s = jnp.where(visible, s, NEG_BIG)
            cols = [s[:, c * 128:(c + 1) * 128] for c in range(nlane)]
            mx = cols[0]
            for c in cols[1:]:
                mx = jnp.maximum(mx, c)
            m_old = m_ref[...]
            m_new = jnp.maximum(m_old, jnp.max(mx, axis=-1, keepdims=True))
            m_ref[...] = m_new
            alphas.append(jnp.exp(m_old - m_new))
            probs.append(jnp.concatenate([jnp.exp((c - m_new).astype(BF16)) for c in cols], axis=1))
        pvs = [jnp.dot(p, vext, preferred_element_type=F32) for p in probs]
        for pv, alpha, (_, _, a_ref) in zip(pvs, alphas, comps):
            a_ref[...] = jnp.concatenate([alpha, alpha], axis=1) * a_ref[...] + pv

    def body(jj, carry):
        update(2 * jj, False)
        update(2 * jj + 1, False)
        return carry

    lax.fori_loop(0, i // 2, body, 0)

    @pl.when(i % 2 == 1)
    def _():
        update(i - 1, False)

    update(i, True)

    a1 = a1_ref[...]
    a2 = a2_ref[...]
    o = a1[:, :B_V] / a1[:, B_V:] - lam_ref[...] * (a2[:, :B_V] / a2[:, B_V:])
    o = o * lax.rsqrt(jnp.mean(o * o, axis=-1, keepdims=True) + EPS) * sub_ref[...]
    o_ref[0] = o.astype(o_ref.dtype)


def _diff_attention(q1, q2, k, v, lam, sub, tq):
    bsz, seq, _ = q1.shape
    nq = seq // tq
    qspec = pl.BlockSpec((1, tq, B_V), lambda b, h, i: (b, i, h))
    kspec = pl.BlockSpec((1, seq, B_V), lambda b, h, i: (b, 0, h))
    pspec = pl.BlockSpec((1, B_V), lambda b, h, i: (0, 0))
    return pl.pallas_call(
        functools.partial(_attn_body, tq=tq),
        out_shape=jax.ShapeDtypeStruct((bsz, seq, B_DIM), BF16),
        grid=(bsz, B_HEADS, nq),
        in_specs=[qspec, qspec, kspec, kspec, pspec, pspec],
        out_specs=pl.BlockSpec((1, tq, B_V), lambda b, h, i: (b, i, h)),
        scratch_shapes=[pltpu.VMEM((tq, 128), F32), pltpu.VMEM((tq, 2 * B_V), F32),
                        pltpu.VMEM((tq, 128), F32), pltpu.VMEM((tq, 2 * B_V), F32)],
        compiler_params=_cparams(("parallel", "parallel", "arbitrary")),
        name="diff_attention",
    )(q1, q2, k, v, lam, sub)


S5_SUB = 8


def _s5_body(u_ref, bd_ref, cd_ref, pw_ref, d_ref, gw_ref, gb_ref, o_ref, xs_ref, carry_ref, *, lt):
    t = pl.program_id(1)

    @pl.when(t == 0)
    def _():
        carry_ref[...] = jnp.zeros_like(carry_ref)

    u_b = u_ref[...]
    u = u_b.astype(F32)
    bu = jnp.dot(u_b, bd_ref[...], preferred_element_type=F32)
    re = bu[:, :C_NS]
    im = bu[:, C_NS:]

    row = lax.broadcasted_iota(jnp.int32, (lt, C_NS), 0) % S5_SUB
    for d in (1, 2, 4):
        pr = pw_ref[d - 1:d, :C_NS]
        pi = pw_ref[d - 1:d, C_NS:]
        keep = row >= d
        sre = jnp.where(keep, pltpu.roll(re, d, 0), 0.0)
        sim = jnp.where(keep, pltpu.roll(im, d, 0), 0.0)
        re, im = re + pr * sre - pi * sim, im + pr * sim + pi * sre
    xs_ref[:, :C_NS] = re
    xs_ref[:, C_NS:] = im

    pw_re = pw_ref[:, :C_NS]
    pw_im = pw_ref[:, C_NS:]

    def group(gidx, carry):
        cr, ci = carry
        rows = pl.ds(pl.multiple_of(gidx * S5_SUB, S5_SUB), S5_SUB)
        nre = xs_ref[rows, :C_NS] + pw_re * cr - pw_im * ci
        nim = xs_ref[rows, C_NS:] + pw_re * ci + pw_im * cr
        xs_ref[rows, :C_NS] = nre
        xs_ref[rows, C_NS:] = nim
        return nre[S5_SUB - 1:S5_SUB, :], nim[S5_SUB - 1:S5_SUB, :]

    cr, ci = lax.fori_loop(0, lt // S5_SUB, group, (carry_ref[0:1, :C_NS], carry_ref[0:1, C_NS:]))
    carry_ref[0:1, :C_NS] = cr
    carry_ref[0:1, C_NS:] = ci

    y = _dot(xs_ref[...], cd_ref[...]) + d_ref[...] * u
    y = _gelu_tanh(y)
    y = y * jax.nn.sigmoid(_dot(y, gw_ref[...]) + gb_ref[...])
    o_ref[...] = y.astype(o_ref.dtype)


def _s5_mixer(z, p, bsz, seq, lt):
    nt = seq // lt
    ublk = Z_U // C_DIM
    full = lambda shape: pl.BlockSpec(shape, lambda b, t: (0, 0))
    return pl.pallas_call(
        functools.partial(_s5_body, lt=lt),
        out_shape=jax.ShapeDtypeStruct((bsz * seq, C_DIM), BF16),
        grid=(bsz, nt),
        in_specs=[pl.BlockSpec((lt, C_DIM), lambda b, t: (b * nt + t, ublk)),
                  full((C_DIM, 2 * C_NS)), full((2 * C_NS, C_DIM)), full((S5_SUB, 2 * C_NS)),
                  full((1, C_DIM)), full((C_DIM, C_DIM)), full((1, C_DIM))],
        out_specs=pl.BlockSpec((lt, C_DIM), lambda b, t: (b * nt + t, 0)),
        scratch_shapes=[pltpu.VMEM((lt, 2 * C_NS), F32), pltpu.VMEM((S5_SUB, 2 * C_NS), F32)],
        compiler_params=_cparams(("parallel", "arbitrary")),
        name="s5_mixer",
    )(z, p["bd"], p["cd"], p["pw"], p["d"], p["glu_w"], p["glu_b"])


def _merge_body(ya_ref, yb_ref, yc_ref, g0_ref, g1_ref, g2_ref, bg_ref, pa_ref, pb_ref, pc_ref, o_ref):
    bg = bg_ref[...]
    m = jax.nn.sigmoid(g0_ref[...].astype(F32) + bg[:, 0:D_MODEL]) * jnp.dot(
        ya_ref[...], pa_ref[...], preferred_element_type=F32)
    m = m + jax.nn.sigmoid(g1_ref[...].astype(F32) + bg[:, D_MODEL:2 * D_MODEL]) * jnp.dot(
        yb_ref[...], pb_ref[...], preferred_element_type=F32)
    m = m + jax.nn.sigmoid(g2_ref[...].astype(F32) + bg[:, 2 * D_MODEL:]) * jnp.dot(
        yc_ref[...], pc_ref[...], preferred_element_type=F32)
    o_ref[...] = m.astype(o_ref.dtype)


def _merge(ya, yb, yc, z, bg, pa, pb, pc, tm):
    t = ya.shape[0]
    full = lambda shape: pl.BlockSpec(shape, lambda i: (0, 0))
    return pl.pallas_call(
        _merge_body,
        out_shape=jax.ShapeDtypeStruct((t, D_MODEL), BF16),
        grid=(t // tm,),
        in_specs=[pl.BlockSpec((tm, A_DIM), lambda i: (i, 0)),
                  pl.BlockSpec((tm, B_DIM), lambda i: (i, 0)),
                  pl.BlockSpec((tm, C_DIM), lambda i: (i, 0)),
                  pl.BlockSpec((tm, D_MODEL), lambda i: (i, 0)),
                  pl.BlockSpec((tm, D_MODEL), lambda i: (i, 1)),
                  pl.BlockSpec((tm, D_MODEL), lambda i: (i, 2)),
                  full((1, N_BRANCH * D_MODEL)),
                  full((A_DIM, D_MODEL)), full((B_DIM, D_MODEL)), full((C_DIM, D_MODEL))],
        out_specs=pl.BlockSpec((tm, D_MODEL), lambda i: (i, 0)),
        compiler_params=_cparams(("parallel",)),
        name="gated_merge",
    )(ya, yb, yc, z, z, z, bg, pa, pb, pc)


def _matmul_res_body(a_ref, w_ref, r_ref, o_ref):
    o_ref[...] = r_ref[...] + jnp.dot(a_ref[...], w_ref[...], preferred_element_type=F32)


def _matmul_res(a, w, res, tm):
    t, kdim = a.shape
    n = w.shape[1]
    return pl.pallas_call(
        _matmul_res_body,
        out_shape=jax.ShapeDtypeStruct((t, n), F32),
        grid=(t // tm,),
        in_specs=[pl.BlockSpec((tm, kdim), lambda i: (i, 0)),
                  pl.BlockSpec((kdim, n), lambda i: (0, 0)),
                  pl.BlockSpec((tm, n), lambda i: (i, 0))],
        out_specs=pl.BlockSpec((tm, n), lambda i: (i, 0)),
        compiler_params=_cparams(("parallel",)),
        name="out_proj_residual",
    )(a, w, res)


FFN_HALO = 16


def _ffn_body(x_ref, xh_ref, g_ref, wv_ref, wg_ref, cw_ref, wd_ref, o_ref, h_ref, acc_ref, act_ref,
              *, tm, blocks_per_seq):
    i = pl.program_id(0)
    j = pl.program_id(1)
    last = pl.num_programs(1) - 1

    def norm(x):
        ms = jnp.mean(x * x, axis=-1, keepdims=True)
        return x * lax.rsqrt(ms + EPS) * g_ref[...]

    def up_gate():
        val = jnp.dot(h_ref[FFN_HALO:, :], wv_ref[...], preferred_element_type=F32)
        gate = jnp.dot(h_ref[...], wg_ref[...], preferred_element_type=F32)
        return val, gate

    def activation(val, gate_ext):
        gate = gate_ext[FFN_HALO:, :]
        row = lax.broadcasted_iota(jnp.int32, gate.shape, 0)
        prev1 = gate_ext[FFN_HALO - 1:FFN_HALO, :]
        prev2 = gate_ext[FFN_HALO - 2:FFN_HALO - 1, :]
        gm1 = jnp.where(row == 0, prev1, pltpu.roll(gate, 1, 0))
        gm2 = jnp.where(row == 0, prev2, jnp.where(row == 1, prev1, pltpu.roll(gate, 2, 0)))
        cw = cw_ref[...]
        conv = cw[0:1, :] * gm2 + cw[1:2, :] * gm1 + cw[2:3, :] * gate
        return (_gelu_tanh(conv) * val).astype(BF16)

    def down(slot):
        return jnp.dot(act_ref[slot], wd_ref[...], preferred_element_type=F32)

    @pl.when(j == 0)
    def _():
        h_ref[FFN_HALO:, :] = norm(x_ref[...]).astype(BF16)
        seq_start = (i % blocks_per_seq) == 0
        h_ref[0:FFN_HALO, :] = jnp.where(seq_start, 0.0, norm(xh_ref[...])).astype(BF16)
        acc_ref[...] = jnp.zeros_like(acc_ref)
        act_ref[0] = activation(*up_gate())

    @pl.when((j > 0) & (j < last))
    def _():
        up = up_gate()
        acc_ref[...] += down((j - 1) % 2)
        act_ref[j % 2] = activation(*up)

    @pl.when(j == last)
    def _():
        o_ref[...] = x_ref[...] + acc_ref[...] + down((j - 1) % 2)


def _ffn(x, g, w_up, conv_w, w_down, seq, tm, tn):
    t, d = x.shape
    nff = D_FF // tn
    halo_per_block = tm // FFN_HALO
    up_blk = lambda j: jnp.minimum(j, nff - 1)
    return pl.pallas_call(
        functools.partial(_ffn_body, tm=tm, blocks_per_seq=seq // tm),
        out_shape=jax.ShapeDtypeStruct((t, d), F32),
        grid=(t // tm, nff + 1),
        in_specs=[pl.BlockSpec((tm, d), lambda i, j: (i, 0)),
                  pl.BlockSpec((FFN_HALO, d), lambda i, j: (jnp.maximum(i * halo_per_block - 1, 0), 0)),
                  pl.BlockSpec((1, d), lambda i, j: (0, 0)),
                  pl.BlockSpec((d, tn), lambda i, j: (0, up_blk(j))),
                  pl.BlockSpec((d, tn), lambda i, j: (0, nff + up_blk(j))),
                  pl.BlockSpec((3, tn), lambda i, j: (0, up_blk(j))),
                  pl.BlockSpec((tn, d), lambda i, j: (jnp.maximum(j - 1, 0), 0))],
        out_specs=pl.BlockSpec((tm, d), lambda i, j: (i, 0)),
        scratch_shapes=[pltpu.VMEM((FFN_HALO + tm, d), BF16),
                        pltpu.VMEM((tm, d), F32), pltpu.VMEM((2, tm, tn), BF16)],
        compiler_params=_cparams(("parallel", "arbitrary")),
        name="conv_glu_ffn",
    )(x, x, g, w_up, w_up, conv_w, w_down)


RW_IN = 3 * A_DIM + LORA_W + LORA_A + LORA_G
B_IN = 3 * B_DIM


def _pad_cols(w, width):
    return jnp.pad(w, ((0, 0), (0, width - w.shape[1])))


def _layout_w_in(w):
    o = 0
    rkv = w[:, o:o + 3 * A_DIM]; o += 3 * A_DIM
    zw = w[:, o:o + LORA_W]; o += LORA_W
    za = w[:, o:o + LORA_A]; o += LORA_A
    zg = w[:, o:o + LORA_G]; o += LORA_G
    qkv = w[:, o:o + B_IN]; o += B_IN
    u = w[:, o:o + C_DIM]; o += C_DIM
    gates = w[:, o:]
    out = jnp.concatenate([gates, rkv, qkv, _pad_cols(zw, LORA_PAD), _pad_cols(za, LORA_PAD), zg, u], axis=1)
    return out.astype(BF16)


def _rwkv_params(mix, w0, w2, a0, a2, g2, k_k, k_a, r_k, ln_w, ln_b):
    row = lambda v: v.reshape(1, -1)
    o = 3 * A_DIM
    mix_l = jnp.concatenate([
        jnp.pad(mix[o:o + LORA_W], (0, LORA_PAD - LORA_W)),
        jnp.pad(mix[o + LORA_W:o + LORA_W + LORA_A], (0, LORA_PAD - LORA_A)),
        mix[o + LORA_W + LORA_A:]])
    pad_rows = lambda m: jnp.pad(m, ((0, LORA_PAD - m.shape[0]), (0, 0)))
    w2 = pad_rows(w2)
    w2h = w2.astype(BF16)
    w2l = (w2 - w2h.astype(F32)).astype(BF16)
    return dict(mix_r=row(mix[0:A_DIM]), mix_k=row(mix[A_DIM:2 * A_DIM]), mix_v=row(mix[2 * A_DIM:3 * A_DIM]),
                mix_l=row(mix_l), w0=row(w0), w2h=w2h, w2l=w2l, a0=row(a0), a2=pad_rows(a2).astype(BF16),
                g2=g2.astype(BF16), k_k=row(k_k), k_a=row(k_a), r_k=row(r_k), ln_w=row(ln_w), ln_b=row(ln_b))


def _s5_params(lam_re, lam_im, log_dt, b_re, b_im, c_re, c_im, d, glu_w, glu_b):
    dt = jnp.exp(log_dt)[:, None]
    er = jnp.exp(lam_re * dt)
    ab_re = er * jnp.cos(lam_im * dt)
    ab_im = er * jnp.sin(lam_im * dt)
    den = lam_re * lam_re + lam_im * lam_im
    nr = ab_re - 1.0
    f_re = (nr * lam_re + ab_im * lam_im) / den
    f_im = (ab_im * lam_re - nr * lam_im) / den
    bb_re = f_re[..., None] * b_re - f_im[..., None] * b_im
    bb_im = f_re[..., None] * b_im + f_im[..., None] * b_re
    eye_g = jnp.eye(C_GROUPS, dtype=F32)
    bd_re = jnp.einsum("gpc,gh->gchp", bb_re, eye_g).reshape(C_DIM, C_NS)
    bd_im = jnp.einsum("gpc,gh->gchp", bb_im, eye_g).reshape(C_DIM, C_NS)
    cd_re = jnp.einsum("gcp,gh->gphc", c_re, eye_g).reshape(C_NS, C_DIM)
    cd_im = jnp.einsum("gcp,gh->gphc", c_im, eye_g).reshape(C_NS, C_DIM)
    pr, pi = [ab_re.reshape(-1)], [ab_im.reshape(-1)]
    for _ in range(S5_SUB - 1):
        pr.append(pr[-1] * pr[0] - pi[-1] * pi[0])
        pi.append(pr[-2] * pi[0] + pi[-1] * pr[0])
    pw = jnp.concatenate([jnp.stack(pr), jnp.stack(pi)], axis=1)
    return dict(bd=jnp.concatenate([bd_re, bd_im], axis=1).astype(BF16),
                cd=jnp.concatenate([cd_re, -cd_im], axis=0).astype(BF16),
                pw=pw, d=d.reshape(1, -1), glu_w=glu_w.astype(BF16), glu_b=glu_b.reshape(1, -1))


def kernel(x, positions, norm_mix, norm_ffn, w_in, b_gate, rw_mix, rw_w0, rw_w2, rw_a0, rw_a2, rw_g2, rw_k_k, rw_k_a, rw_r_k, rw_ln_w, rw_ln_b, da_lq1, da_lk1, da_lq2, da_lk2, da_subln, s5_lam_re, s5_lam_im, s5_log_dt, s5_b_re, s5_b_im, s5_c_re, s5_c_im, s5_d, s5_glu_w, s5_glu_b, proj_a, proj_b, proj_c, w_out, ffn_up, ffn_conv, ffn_down, norm_final):
    bsz, seq, d = x.shape
    t = bsz * seq
    tm_proj = min(1024, seq)
    tm_ffn = min(512, seq)
    tm_small = min(256, seq)
    tq = min(512, seq)
    lt = min(256, seq)

    inv_freq = ROPE_THETA ** (-jnp.arange(0, B_QK, 2, dtype=F32) / B_QK)
    ang = positions.astype(F32)[..., None] * inv_freq
    cos, sin = jnp.cos(ang), jnp.sin(ang)
    cos_t = jnp.concatenate([cos, cos, cos, cos], axis=-1).reshape(t, 128)
    sin_t = jnp.concatenate([-sin, sin, -sin, sin], axis=-1).reshape(t, 128)

    xf = x.reshape(t, d)
    for l in range(DEPTH):
        z = _norm_matmul(xf, norm_mix[l].reshape(1, d), _layout_w_in(w_in[l]), tm_proj, 512)

        rp = _rwkv_params(rw_mix[l], rw_w0[l], rw_w2[l], rw_a0[l], rw_a2[l], rw_g2[l], rw_k_k[l],
                          rw_k_a[l], rw_r_k[l].reshape(-1), rw_ln_w[l], rw_ln_b[l])
        y_a = _rwkv_mixer(z, rp, bsz, seq)

        lam_init = 0.8 - 0.6 * math.exp(-0.3 * l)
        lam = jnp.exp(jnp.sum(da_lq1[l] * da_lk1[l])) - jnp.exp(jnp.sum(da_lq2[l] * da_lk2[l])) + lam_init
        q1, q2, kr, vr = _rope_prep(z, cos_t, sin_t, tm_ffn)
        shp = (bsz, seq, B_DIM)
        y_b = _diff_attention(q1.reshape(shp), q2.reshape(shp), kr.reshape(shp), vr.reshape(shp),
                              jnp.full((1, B_V), lam, F32),
                              (da_subln[l] * (1.0 - lam_init)).reshape(1, B_V), tq).reshape(t, B_DIM)

        sp = _s5_params(s5_lam_re[l], s5_lam_im[l], s5_log_dt[l], s5_b_re[l], s5_b_im[l], s5_c_re[l],
                        s5_c_im[l], s5_d[l], s5_glu_w[l], s5_glu_b[l])
        y_c = _s5_mixer(z, sp, bsz, seq, lt)

        merged = _merge(y_a, y_b, y_c, z, b_gate[l].reshape(1, -1), proj_a[l].astype(BF16),
                        proj_b[l].astype(BF16), proj_c[l].astype(BF16), tm_small)
        xf = _matmul_res(merged, w_out[l].astype(BF16), xf, tm_ffn)
        xf = _ffn(xf, norm_ffn[l].reshape(1, d), ffn_up[l].astype(BF16), ffn_conv[l],
                  ffn_down[l].astype(BF16), seq, tm_ffn, 512)
    return _final_norm(xf, norm_final.reshape(1, d), tm_ffn).reshape(bsz, seq, d)
```

```python
import functools
import math

import jax
import jax.numpy as jnp
import numpy as np
from jax import lax
from jax.experimental import pallas as pl
from jax.experimental.pallas import tpu as pltpu

F32 = jnp.float32
BF16 = jnp.bfloat16

D_MODEL = 2048
DEPTH = 4
CHUNK = 64
EPS = 1e-6
ROPE_THETA = 10000.0

A_HEAD = 64
A_DIM = 768
A_HEADS = 12
LORA_W = 96
LORA_A = 96
LORA_G = 256
LORA_PAD = 128
GN_EPS = 64e-5

B_QK = 64
B_V = 128
B_DIM = 768
B_HEADS = 6

C_GROUP = 16
C_DIM = 512
C_GROUPS = 32
C_STATE = 64
C_NS = C_GROUPS * C_STATE

N_BRANCH = 3
D_FF = 5632

Z_GATE = 0
Z_RKV = N_BRANCH * D_MODEL
Z_QKV = Z_RKV + 3 * A_DIM
Z_LORA = Z_QKV + 3 * B_DIM
LORA_IN = 2 * LORA_PAD + LORA_G
Z_U = Z_LORA + LORA_IN
Z_COLS = Z_U + C_DIM

VMEM_LIMIT = 56 * 1024 * 1024
NEG_BIG = -1e30


def _cparams(sem):
    return pltpu.CompilerParams(dimension_semantics=sem, vmem_limit_bytes=VMEM_LIMIT)


def _gelu_tanh(x):
    return 0.5 * x * (1.0 + jnp.tanh(math.sqrt(2.0 / math.pi) * (x + 0.044715 * (x * x * x))))


def _dot(a, b):
    return jnp.dot(a.astype(BF16), b.astype(BF16), preferred_element_type=F32)


def _dot_f32(a, b):
    return jnp.dot(a, b, preferred_element_type=F32, precision=lax.Precision.HIGHEST)


def _dot_nt(a, b):
    return lax.dot_general(a.astype(BF16), b.astype(BF16), (((1,), (1,)), ((), ())),
                           preferred_element_type=F32)


def _dot_tn(a, b, precision=None):
    return lax.dot_general(a, b, (((0,), (0,)), ((), ())), preferred_element_type=F32,
                           precision=precision)


def _norm_matmul_body(x_ref, g_ref, w_ref, o_ref, h_ref):
    @pl.when(pl.program_id(1) == 0)
    def _():
        x = x_ref[...]
        ms = jnp.mean(x * x, axis=-1, keepdims=True)
        h_ref[...] = (x * lax.rsqrt(ms + EPS) * g_ref[...]).astype(BF16)

    o_ref[...] = jnp.dot(h_ref[...], w_ref[...], preferred_element_type=F32).astype(o_ref.dtype)


def _norm_matmul(x, g, w, tm, tn):
    t, d = x.shape
    n = w.shape[1]
    return pl.pallas_call(
        _norm_matmul_body,
        out_shape=jax.ShapeDtypeStruct((t, n), BF16),
        grid=(t // tm, n // tn),
        in_specs=[pl.BlockSpec((tm, d), lambda i, j: (i, 0)),
                  pl.BlockSpec((1, d), lambda i, j: (0, 0)),
                  pl.BlockSpec((d, tn), lambda i, j: (0, j))],
        out_specs=pl.BlockSpec((tm, tn), lambda i, j: (i, j)),
        scratch_shapes=[pltpu.VMEM((tm, d), BF16)],
        compiler_params=_cparams(("parallel", "arbitrary")),
        name="norm_in_proj",
    )(x, g, w)


def _final_norm_body(x_ref, g_ref, o_ref):
    x = x_ref[...]
    ms = jnp.mean(x * x, axis=-1, keepdims=True)
    o_ref[...] = x * lax.rsqrt(ms + EPS) * g_ref[...]


def _final_norm(x, g, tm):
    t, d = x.shape
    return pl.pallas_call(
        _final_norm_body,
        out_shape=jax.ShapeDtypeStruct((t, d), F32),
        grid=(t // tm,),
        in_specs=[pl.BlockSpec((tm, d), lambda i: (i, 0)),
                  pl.BlockSpec((1, d), lambda i: (0, 0))],
        out_specs=pl.BlockSpec((tm, d), lambda i: (i, 0)),
        compiler_params=_cparams(("parallel",)),
        name="final_norm",
    )(x, g)


RW_LANES = 2 * A_HEAD
RW_ROWS = 2 * CHUNK
RW_PAIRS = A_HEADS // 2
RW_NC = 4
RW_SC = 2
RW_HALO = 16


def _shift_rows(x, prev_row):
    row = lax.broadcasted_iota(jnp.int32, x.shape, 0)
    return jnp.where(row == 0, prev_row, pltpu.roll(x, 1, 0))


def _split_bf16(x, pieces):
    out = []
    for _ in range(pieces - 1):
        h = x.astype(BF16)
        out.append(h)
        x = x - h.astype(F32)
    out.append(x.astype(BF16))
    return out


def _pair_masks():
    lane_r = lax.broadcasted_iota(jnp.int32, (RW_LANES, RW_LANES), 0)
    lane_c = lax.broadcasted_iota(jnp.int32, (RW_LANES, RW_LANES), 1)
    same_head = (lane_r // A_HEAD) == (lane_c // A_HEAD)
    head0 = lax.broadcasted_iota(jnp.int32, (CHUNK, RW_LANES), 1) < A_HEAD
    return same_head, head0


def _stack_heads(x, head0):
    return jnp.concatenate([jnp.where(head0, x, 0.0), jnp.where(head0, 0.0, x)], axis=0)


def _unstack_heads(x):
    return x[0:CHUNK, :] + x[CHUNK:, :]


def _rwkv_prep_body(zr_ref, zk_ref, zv_ref, zl_ref, hr_ref, hk_ref, hv_ref, hl_ref,
                    mr_ref, mk_ref, mv_ref, ml_ref,
                    w0_ref, w2h_ref, w2l_ref, a0_ref, a2_ref, g2_ref, kk_ref, ka_ref, rk_ref,
                    w_ref, u0_ref, rt_ref, y0_ref, lrb_ref, kend_ref, bend_ref, v_ref, gam_ref,
                    g_ref, bonus_ref):
    first = pl.program_id(1) == 0

    def token_shift(z_ref, halo_ref, mix_ref):
        z = z_ref[...].astype(F32)
        prev = jnp.where(first, 0.0, halo_ref[RW_HALO - 1:RW_HALO, :].astype(F32))
        return z + (_shift_rows(z, prev) - z) * mix_ref[...]

    r = token_shift(zr_ref, hr_ref, mr_ref)
    k = token_shift(zk_ref, hk_ref, mk_ref)
    v = token_shift(zv_ref, hv_ref, mv_ref)
    zl = token_shift(zl_ref, hl_ref, ml_ref)
    zw = zl[:, 0:LORA_PAD]
    za = zl[:, LORA_PAD:2 * LORA_PAD]
    zg = zl[:, 2 * LORA_PAD:]

    th_h, th_l = _split_bf16(jnp.tanh(zw), 2)
    w2h = w2h_ref[...]
    pre_w = (w0_ref[...] + jnp.dot(th_h, w2h, preferred_element_type=F32)
             + jnp.dot(th_h, w2l_ref[...], preferred_element_type=F32)
             + jnp.dot(th_l, w2h, preferred_element_type=F32))
    neg = -pre_w
    softplus = jnp.maximum(neg, 0.0) + jnp.log(1.0 + jnp.exp(-jnp.abs(neg)))
    logw = -jnp.exp(-softplus - 0.5)
    a = jax.nn.sigmoid(a0_ref[...] + _dot(za, a2_ref[...]))
    g = _dot(jax.nn.sigmoid(zg), g2_ref[...])
    g_ref[...] = g.astype(g_ref.dtype)
    v_ref[...] = v.astype(v_ref.dtype)

    rows = RW_NC * CHUNK
    ti = lax.broadcasted_iota(jnp.int32, (rows, rows), 0)
    tj = lax.broadcasted_iota(jnp.int32, (rows, rows), 1)
    tri = ((tj <= ti) & ((ti // CHUNK) == (tj // CHUNK))).astype(BF16)
    lg = sum(jnp.dot(tri, piece, preferred_element_type=F32) for piece in _split_bf16(logw, 3))
    rss = [slice(ci * CHUNK, (ci + 1) * CHUNK) for ci in range(RW_NC)]
    lg_last = [lg[rs.stop - 1:rs.stop, :] for rs in rss]
    for ci in range(RW_NC):
        gam_ref[ci] = jnp.exp(lg_last[ci])

    kk_raw = k * kk_ref[...]
    k = k * (1.0 + (a - 1.0) * ka_ref[...])
    e_pos = jnp.exp(lg)
    e_neg = jnp.exp(-lg)
    e_prev = jnp.exp(lg - logw)
    e_end = jnp.concatenate([jnp.exp(last - lg[rs, :]) for last, rs in zip(lg_last, rss)], axis=0)
    rt_ref[...] = (r * e_pos).astype(rt_ref.dtype)
    kend_ref[...] = (k * e_end).astype(kend_ref.dtype)
    k_t = k * e_neg
    rkr = r * k * rk_ref[...]

    same_head, head0 = _pair_masks()
    head_ones = same_head.astype(BF16)
    sr = lax.broadcasted_iota(jnp.int32, (RW_ROWS, RW_ROWS), 0)
    sc = lax.broadcasted_iota(jnp.int32, (RW_ROWS, RW_ROWS), 1)
    same_blk = (sr // CHUNK) == (sc // CHUNK)
    strict = same_blk & ((sr % CHUNK) > (sc % CHUNK))
    incl = same_blk & ((sr % CHUNK) >= (sc % CHUNK))
    eye = (sr == sc).astype(F32)
    level_masks = []
    s = 1
    while s < CHUNK:
        level_masks.append(((sr // (2 * s)) == (sc // (2 * s))) & (((sr // s) % 2) == 1) & (((sc // s) % 2) == 0))
        s *= 2

    at = [(rs, slice(p * RW_LANES, (p + 1) * RW_LANES)) for rs in rss for p in range(RW_PAIRS)]
    chains = range(len(at))
    kk_ss = [jnp.dot((kk_raw[ix] * kk_raw[ix]).astype(BF16), head_ones, preferred_element_type=F32) for ix in at]
    kk = [kk_raw[ix] / jnp.maximum(jnp.sqrt(ss), 1e-12) for ix, ss in zip(at, kk_ss)]
    kka = [kk[c] * a[at[c]] for c in chains]
    for c in chains:
        bend_ref[at[c]] = (kka[c] * e_end[at[c]]).astype(bend_ref.dtype)
    hsum = [jnp.dot(rkr[ix].astype(BF16), head_ones, preferred_element_type=F32) for ix in at]
    for c in chains:
        bonus_ref[at[c]] = (hsum[c] * v[at[c]]).astype(bonus_ref.dtype)

    a_s = [_stack_heads(-kk[c] * e_prev[at[c]], head0).astype(BF16) for c in chains]
    b_s = [_stack_heads(kka[c] * e_neg[at[c]], head0).astype(BF16) for c in chains]
    k_s = [_stack_heads(k_t[ix], head0).astype(BF16) for ix in at]
    r_s = [_stack_heads(r[ix] * e_pos[ix], head0).astype(BF16) for ix in at]
    v_s = [_stack_heads(v[ix], head0).astype(BF16) for ix in at]

    l_ab = [jnp.where(strict, _dot_nt(a_s[c], b_s[c]), 0.0) for c in chains]
    l_ak = [jnp.where(strict, _dot_nt(a_s[c], k_s[c]), 0.0).astype(BF16) for c in chains]
    l_rk = [jnp.where(incl, _dot_nt(r_s[c], k_s[c]), 0.0).astype(BF16) for c in chains]
    l_rb = [jnp.where(incl, _dot_nt(r_s[c], b_s[c]), 0.0) for c in chains]
    for c in chains:
        lrb_ref[at[c]] = _unstack_heads(l_rb[c]).astype(lrb_ref.dtype)
    y0 = [_dot(l_rk[c], v_s[c]) for c in chains]
    for c in chains:
        y0_ref[at[c]] = _unstack_heads(y0[c])
    x0 = [_dot(l_ak[c], v_s[c]) for c in chains]

    t_inv = [eye + jnp.where(level_masks[0], l_ab[c], 0.0) for c in chains]
    for m in level_masks[1:]:
        lt = [_dot(jnp.where(m, l_ab[c], 0.0), t_inv[c]) for c in chains]
        t_inv = [t_inv[c] + _dot(t_inv[c], lt[c]) for c in chains]

    w = [_dot(t_inv[c], a_s[c]) for c in chains]
    u0 = [_dot(t_inv[c], x0[c]) for c in chains]
    for c in chains:
        w_ref[at[c]] = _unstack_heads(w[c]).astype(w_ref.dtype)
        u0_ref[at[c]] = _unstack_heads(u0[c])


def _rwkv_scan_body(w_ref, u0_ref, rt_ref, y0_ref, lrb_ref, kend_ref, bend_ref, v_ref, gam_ref,
                    g_ref, bonus_ref, lnw_ref, lnb_ref, o_ref, s_ref):
    @pl.when(pl.program_id(1) == 0)
    def _():
        s_ref[...] = jnp.zeros_like(s_ref)

    same_head, head0 = _pair_masks()
    head_avg = (same_head.astype(F32) * (1.0 / A_HEAD)).astype(BF16)

    pairs = range(RW_PAIRS)
    sls = [slice(p * RW_LANES, (p + 1) * RW_LANES) for p in pairs]
    s = [s_ref[p] for p in pairs]
    for ci in range(RW_SC):
        rs = slice(ci * CHUNK, (ci + 1) * CHUNK)
        gam = gam_ref[ci]
        ws = [lax.dot_general(jnp.concatenate([w_ref[rs, sl], rt_ref[rs, sl]], axis=0), s[p].astype(BF16),
                              (((1,), (1,)), ((), ())), preferred_element_type=F32)
              for p, sl in zip(pairs, sls)]
        u = [ws[p][0:CHUNK, :] + u0_ref[rs, sls[p]] for p in pairs]
        s_add = [_dot_tn(jnp.concatenate([v_ref[rs, sl], u[p].astype(BF16)], axis=0),
                         jnp.concatenate([kend_ref[rs, sl], bend_ref[rs, sl]], axis=0))
                 for p, sl in zip(pairs, sls)]
        s = [s[p] * gam[:, sls[p]] + jnp.where(same_head, s_add[p], 0.0) for p in pairs]
        y = [ws[p][CHUNK:, :] + y0_ref[rs, sls[p]]
             + jnp.dot(lrb_ref[rs, sls[p]], _stack_heads(u[p], head0).astype(BF16), preferred_element_type=F32)
             for p in pairs]

        y_hl = [_split_bf16(y[p], 2) for p in pairs]
        mu = [jnp.dot(h, head_avg, preferred_element_type=F32) + jnp.dot(l, head_avg, preferred_element_type=F32)
              for h, l in y_hl]
        d = [y[p] - mu[p] for p in pairs]
        var = [jnp.dot((d[p] * d[p]).astype(BF16), head_avg, preferred_element_type=F32) for p in pairs]
        for p, sl in zip(pairs, sls):
            yn = d[p] * lax.rsqrt(var[p] + GN_EPS) * lnw_ref[:, sl] + lnb_ref[:, sl]
            o_ref[rs, sl] = ((yn + bonus_ref[rs, sl]) * g_ref[rs, sl].astype(F32)).astype(o_ref.dtype)
    for p in pairs:
        s_ref[p] = s[p]


def _rwkv_mixer(z, p, bsz, seq):
    nchunk = seq // CHUNK
    t = bsz * seq
    rblk = Z_RKV // A_DIM
    lblk = Z_LORA // LORA_IN
    prows = RW_NC * CHUNK
    nprep = seq // prows
    halo_per_step = prows // RW_HALO

    def zspec(width, blk):
        return pl.BlockSpec((prows, width), lambda b, c: (b * nprep + c, blk))

    def hspec(width, blk):
        return pl.BlockSpec((RW_HALO, width),
                            lambda b, c: (jnp.maximum((b * nprep + c) * halo_per_step - 1, 0), blk))

    full = lambda shape: pl.BlockSpec(shape, lambda b, c: (0, 0))
    ptok = pl.BlockSpec((prows, A_DIM), lambda b, c: (b * nprep + c, 0))
    pgam = pl.BlockSpec((RW_NC, 1, A_DIM), lambda b, c: (b * nprep + c, 0, 0))
    nscan = nchunk // RW_SC
    tok = pl.BlockSpec((RW_SC * CHUNK, A_DIM), lambda b, c: (b * nscan + c, 0))
    gam_spec = pl.BlockSpec((RW_SC, 1, A_DIM), lambda b, c: (b * nscan + c, 0, 0))
    bf = jax.ShapeDtypeStruct((t, A_DIM), BF16)
    f32 = jax.ShapeDtypeStruct((t, A_DIM), F32)
    gam_shape = jax.ShapeDtypeStruct((bsz * nchunk, 1, A_DIM), F32)

    prep = pl.pallas_call(
        _rwkv_prep_body,
        out_shape=(bf, f32, bf, f32, bf, bf, bf, bf, gam_shape, bf, f32),
        grid=(bsz, nprep),
        in_specs=[zspec(A_DIM, rblk), zspec(A_DIM, rblk + 1), zspec(A_DIM, rblk + 2), zspec(LORA_IN, lblk),
                  hspec(A_DIM, rblk), hspec(A_DIM, rblk + 1), hspec(A_DIM, rblk + 2), hspec(LORA_IN, lblk),
                  full((1, A_DIM)), full((1, A_DIM)), full((1, A_DIM)), full((1, LORA_IN)),
                  full((1, A_DIM)), full((LORA_PAD, A_DIM)), full((LORA_PAD, A_DIM)),
                  full((1, A_DIM)), full((LORA_PAD, A_DIM)), full((LORA_G, A_DIM)),
                  full((1, A_DIM)), full((1, A_DIM)), full((1, A_DIM))],
        out_specs=(ptok, ptok, ptok, ptok, ptok, ptok, ptok, ptok, pgam, ptok, ptok),
        compiler_params=_cparams(("parallel", "parallel")),
        name="rwkv7_prep",
    )(z, z, z, z, z, z, z, z,
      p["mix_r"], p["mix_k"], p["mix_v"], p["mix_l"],
      p["w0"], p["w2h"], p["w2l"], p["a0"], p["a2"], p["g2"], p["k_k"], p["k_a"], p["r_k"])

    return pl.pallas_call(
        _rwkv_scan_body,
        out_shape=bf,
        grid=(bsz, nscan),
        in_specs=[tok, tok, tok, tok, tok, tok, tok, tok, gam_spec, tok, tok,
                  full((1, A_DIM)), full((1, A_DIM))],
        out_specs=tok,
        scratch_shapes=[pltpu.VMEM((RW_PAIRS, RW_LANES, RW_LANES), F32)],
        compiler_params=_cparams(("parallel", "arbitrary")),
        name="rwkv7_scan",
    )(*prep, p["ln_w"], p["ln_b"])


def _rope_body(zq_ref, zk_ref, zv_ref, cos_ref, sin_ref, q1_ref, q2_ref, k_ref, v_ref):
    reps = B_DIM // 128
    cos = jnp.concatenate([cos_ref[...]] * reps, axis=1)
    sin = jnp.concatenate([sin_ref[...]] * reps, axis=1)
    lane = lax.broadcasted_iota(jnp.int32, cos.shape, 1)
    first_half = (lane % B_QK) < (B_QK // 2)
    comp0 = (lane % B_V) < B_QK

    def rope(x):
        partner = jnp.where(first_half, pltpu.roll(x, B_DIM - B_QK // 2, 1), pltpu.roll(x, B_QK // 2, 1))
        return x * cos + partner * sin

    q = rope(zq_ref[...].astype(F32)) * (B_QK ** -0.5)
    q1_ref[...] = jnp.where(comp0, q, 0.0).astype(BF16)
    q2_ref[...] = jnp.where(comp0, 0.0, q).astype(BF16)
    k_ref[...] = rope(zk_ref[...].astype(F32)).astype(BF16)
    v_ref[...] = zv_ref[...]


def _rope_prep(z, cos_t, sin_t, tm):
    t = z.shape[0]
    qblk = Z_QKV // B_DIM
    out = jax.ShapeDtypeStruct((t, B_DIM), BF16)
    ospec = pl.BlockSpec((tm, B_DIM), lambda i: (i, 0))
    return pl.pallas_call(
        _rope_body,
        out_shape=(out, out, out, out),
        grid=(t // tm,),
        in_specs=[pl.BlockSpec((tm, B_DIM), lambda i: (i, qblk)),
                  pl.BlockSpec((tm, B_DIM), lambda i: (i, qblk + 1)),
                  pl.BlockSpec((tm, B_DIM), lambda i: (i, qblk + 2)),
                  pl.BlockSpec((tm, 128), lambda i: (i, 0)),
                  pl.BlockSpec((tm, 128), lambda i: (i, 0))],
        out_specs=(ospec, ospec, ospec, ospec),
        compiler_params=_cparams(("parallel",)),
        name="rope_prep",
    )(z, z, z, cos_t, sin_t)


def _attn_body(q1_ref, q2_ref, k_ref, v_ref, lam_ref, sub_ref, o_ref,
               m1_ref, a1_ref, m2_ref, a2_ref, *, tq):
    i = pl.program_id(2)
    nlane = tq // 128

    m1_ref[...] = jnp.full_like(m1_ref, NEG_BIG)
    m2_ref[...] = jnp.full_like(m2_ref, NEG_BIG)
    a1_ref[...] = jnp.zeros_like(a1_ref)
    a2_ref[...] = jnp.zeros_like(a2_ref)
    ones = jnp.ones((tq, B_V), BF16)

    def update(j, masked):
        rows = pl.ds(pl.multiple_of(j * tq, tq), tq)
        kb = k_ref[0, rows, :]
        vext = jnp.concatenate([v_ref[0, rows, :], ones], axis=1)
        if masked:
            row = lax.broadcasted_iota(jnp.int32, (tq, tq), 0)
            col = lax.broadcasted_iota(jnp.int32, (tq, tq), 1)
            visible = (col // CHUNK) <= (row // CHUNK)
        comps = ((q1_ref, m1_ref, a1_ref), (q2_ref, m2_ref, a2_ref))
        scores = [lax.dot_general(q_ref[0], kb, (((1,), (1,)), ((), ())), preferred_element_type=F32)
                  for q_ref, _, _ in comps]
        probs, alphas = [], []
        for s, (_, m_ref, _) in zip(scores, comps):
            if masked:
                s = jnp.where(visible, s, NEG_BIG)
            cols = [s[:, c * 128:(c + 1) * 128] for c in range(nlane)]
            mx = cols[0]
            for c in cols[1:]:
                mx = jnp.maximum(mx, c)
            m_old = m_ref[...]
            m_new = jnp.maximum(m_old, jnp.max(mx, axis=-1, keepdims=True))
            m_ref[...] = m_new
            alphas.append(jnp.exp(m_old - m_new))
            probs.append(jnp.concatenate([jnp.exp((c - m_new).astype(BF16)) for c in cols], axis=1))
        pvs = [jnp.dot(p, vext, preferred_element_type=F32) for p in probs]
        for pv, alpha, (_, _, a_ref) in zip(pvs, alphas, comps):
            a_ref[...] = jnp.concatenate([alpha, alpha], axis=1) * a_ref[...] + pv

    def body(jj, carry):
        update(2 * jj, False)
        update(2 * jj + 1, False)
        return carry

    lax.fori_loop(0, i // 2, body, 0)

    @pl.when(i % 2 == 1)
    def _():
        update(i - 1, False)

    update(i, True)

    a1 = a1_ref[...]
    a2 = a2_ref[...]
    o = a1[:, :B_V] / a1[:, B_V:] - lam_ref[...] * (a2[:, :B_V] / a2[:, B_V:])
    o = o * lax.rsqrt(jnp.mean(o * o, axis=-1, keepdims=True) + EPS) * sub_ref[...]
    o_ref[0] = o.astype(o_ref.dtype)


def _diff_attention(q1, q2, k, v, lam, sub, tq):
    bsz, seq, _ = q1.shape
    nq = seq // tq
    qspec = pl.BlockSpec((1, tq, B_V), lambda b, h, i: (b, i, h))
    kspec = pl.BlockSpec((1, seq, B_V), lambda b, h, i: (b, 0, h))
    pspec = pl.BlockSpec((1, B_V), lambda b, h, i: (0, 0))
    return pl.pallas_call(
        functools.partial(_attn_body, tq=tq),
        out_shape=jax.ShapeDtypeStruct((bsz, seq, B_DIM), BF16),
        grid=(bsz, B_HEADS, nq),
        in_specs=[qspec, qspec, kspec, kspec, pspec, pspec],
        out_specs=pl.BlockSpec((1, tq, B_V), lambda b, h, i: (b, i, h)),
        scratch_shapes=[pltpu.VMEM((tq, 128), F32), pltpu.VMEM((tq, 2 * B_V), F32),
                        pltpu.VMEM((tq, 128), F32), pltpu.VMEM((tq, 2 * B_V), F32)],
        compiler_params=_cparams(("parallel", "parallel", "arbitrary")),
        name="diff_attention",
    )(q1, q2, k, v, lam, sub)


S5_SUB = 8


def _s5_body(u_ref, bd_ref, cd_ref, pw_ref, d_ref, gw_ref, gb_ref, o_ref, xs_ref, carry_ref, *, lt):
    t = pl.program_id(1)

    @pl.when(t == 0)
    def _():
        carry_ref[...] = jnp.zeros_like(carry_ref)

    u_b = u_ref[...]
    u = u_b.astype(F32)
    bu = jnp.dot(u_b, bd_ref[...], preferred_element_type=F32)
    re = bu[:, :C_NS]
    im = bu[:, C_NS:]

    row = lax.broadcasted_iota(jnp.int32, (lt, C_NS), 0) % S5_SUB
    for d in (1, 2, 4):
        pr = pw_ref[d - 1:d, :C_NS]
        pi = pw_ref[d - 1:d, C_NS:]
        keep = row >= d
        sre = jnp.where(keep, pltpu.roll(re, d, 0), 0.0)
        sim = jnp.where(keep, pltpu.roll(im, d, 0), 0.0)
        re, im = re + pr * sre - pi * sim, im + pr * sim + pi * sre
    xs_ref[:, :C_NS] = re
    xs_ref[:, C_NS:] = im

    pw_re = pw_ref[:, :C_NS]
    pw_im = pw_ref[:, C_NS:]

    def group(gidx, carry):
        cr, ci = carry
        rows = pl.ds(pl.multiple_of(gidx * S5_SUB, S5_SUB), S5_SUB)
        nre = xs_ref[rows, :C_NS] + pw_re * cr - pw_im * ci
        nim = xs_ref[rows, C_NS:] + pw_re * ci + pw_im * cr
        xs_ref[rows, :C_NS] = nre
        xs_ref[rows, C_NS:] = nim
        return nre[S5_SUB - 1:S5_SUB, :], nim[S5_SUB - 1:S5_SUB, :]

    cr, ci = lax.fori_loop(0, lt // S5_SUB, group, (carry_ref[0:1, :C_NS], carry_ref[0:1, C_NS:]))
    carry_ref[0:1, :C_NS] = cr
    carry_ref[0:1, C_NS:] = ci

    y = _dot(xs_ref[...], cd_ref[...]) + d_ref[...] * u
    y = _gelu_tanh(y)
    y = y * jax.nn.sigmoid(_dot(y, gw_ref[...]) + gb_ref[...])
    o_ref[...] = y.astype(o_ref.dtype)


def _s5_mixer(z, p, bsz, seq, lt):
    nt = seq // lt
    ublk = Z_U // C_DIM
    full = lambda shape: pl.BlockSpec(shape, lambda b, t: (0, 0))
    return pl.pallas_call(
        functools.partial(_s5_body, lt=lt),
        out_shape=jax.ShapeDtypeStruct((bsz * seq, C_DIM), BF16),
        grid=(bsz, nt),
        in_specs=[pl.BlockSpec((lt, C_DIM), lambda b, t: (b * nt + t, ublk)),
                  full((C_DIM, 2 * C_NS)), full((2 * C_NS, C_DIM)), full((S5_SUB, 2 * C_NS)),
                  full((1, C_DIM)), full((C_DIM, C_DIM)), full((1, C_DIM))],
        out_specs=pl.BlockSpec((lt, C_DIM), lambda b, t: (b * nt + t, 0)),
        scratch_shapes=[pltpu.VMEM((lt, 2 * C_NS), F32), pltpu.VMEM((S5_SUB, 2 * C_NS), F32)],
        compiler_params=_cparams(("parallel", "arbitrary")),
        name="s5_mixer",
    )(z, p["bd"], p["cd"], p["pw"], p["d"], p["glu_w"], p["glu_b"])


def _merge_out_body(ya_ref, yb_ref, yc_ref, g0_ref, g1_ref, g2_ref, bg_ref, pa_ref, pb_ref, pc_ref,
                    wo_ref, x_ref, o_ref):
    bg = bg_ref[...]
    m = jax.nn.sigmoid(g0_ref[...].astype(F32) + bg[:, 0:D_MODEL]) * jnp.dot(
        ya_ref[...], pa_ref[...], preferred_element_type=F32)
    m = m + jax.nn.sigmoid(g1_ref[...].astype(F32) + bg[:, D_MODEL:2 * D_MODEL]) * jnp.dot(
        yb_ref[...], pb_ref[...], preferred_element_type=F32)
    m = m + jax.nn.sigmoid(g2_ref[...].astype(F32) + bg[:, 2 * D_MODEL:]) * jnp.dot(
        yc_ref[...], pc_ref[...], preferred_element_type=F32)
    o_ref[...] = x_ref[...] + jnp.dot(m.astype(BF16), wo_ref[...], preferred_element_type=F32)


def _merge_out(ya, yb, yc, z, bg, pa, pb, pc, wo, x, tm):
    t = ya.shape[0]
    full = lambda shape: pl.BlockSpec(shape, lambda i: (0, 0))
    return pl.pallas_call(
        _merge_out_body,
        out_shape=jax.ShapeDtypeStruct((t, D_MODEL), F32),
        grid=(t // tm,),
        in_specs=[pl.BlockSpec((tm, A_DIM), lambda i: (i, 0)),
                  pl.BlockSpec((tm, B_DIM), lambda i: (i, 0)),
                  pl.BlockSpec((tm, C_DIM), lambda i: (i, 0)),
                  pl.BlockSpec((tm, D_MODEL), lambda i: (i, 0)),
                  pl.BlockSpec((tm, D_MODEL), lambda i: (i, 1)),
                  pl.BlockSpec((tm, D_MODEL), lambda i: (i, 2)),
                  full((1, N_BRANCH * D_MODEL)),
                  full((A_DIM, D_MODEL)), full((B_DIM, D_MODEL)), full((C_DIM, D_MODEL)),
                  full((D_MODEL, D_MODEL)),
                  pl.BlockSpec((tm, D_MODEL), lambda i: (i, 0))],
        out_specs=pl.BlockSpec((tm, D_MODEL), lambda i: (i, 0)),
        compiler_params=_cparams(("parallel",)),
        name="merge_out_proj",
    )(ya, yb, yc, z, z, z, bg, pa, pb, pc, wo, x)


FFN_HALO = 16


def _ffn_body(x_ref, xh_ref, g_ref, wv_ref, wg_ref, cw_ref, wd_ref, o_ref, h_ref, act_ref,
              *, tm, blocks_per_seq):
    i = pl.program_id(0)
    j = pl.program_id(1)
    last = pl.num_programs(1) - 1

    def norm(x):
        ms = jnp.mean(x * x, axis=-1, keepdims=True)
        return x * lax.rsqrt(ms + EPS) * g_ref[...]

    def up_gate():
        val = jnp.dot(h_ref[FFN_HALO:, :], wv_ref[...], preferred_element_type=F32)
        gate = jnp.dot(h_ref[...], wg_ref[...], preferred_element_type=F32)
        return val, gate

    def activation(val, gate_ext):
        gate = gate_ext[FFN_HALO:, :]
        row = lax.broadcasted_iota(jnp.int32, gate.shape, 0)
        prev1 = gate_ext[FFN_HALO - 1:FFN_HALO, :]
        prev2 = gate_ext[FFN_HALO - 2:FFN_HALO - 1, :]
        gm1 = jnp.where(row == 0, prev1, pltpu.roll(gate, 1, 0))
        gm2 = jnp.where(row == 0, prev2, jnp.where(row == 1, prev1, pltpu.roll(gate, 2, 0)))
        cw = cw_ref[...]
        conv = cw[0:1, :] * gm2 + cw[1:2, :] * gm1 + cw[2:3, :] * gate
        return (_gelu_tanh(conv) * val).astype(BF16)

    def down(slot):
        return jnp.dot(act_ref[slot], wd_ref[...], preferred_element_type=F32)

    @pl.when(j == 0)
    def _():
        h_ref[FFN_HALO:, :] = norm(x_ref[...]).astype(BF16)
        seq_start = (i % blocks_per_seq) == 0
        h_ref[0:FFN_HALO, :] = jnp.where(seq_start, 0.0, norm(xh_ref[...])).astype(BF16)
        o_ref[...] = x_ref[...]
        act_ref[0] = activation(*up_gate())

    @pl.when((j > 0) & (j < last))
    def _():
        up = up_gate()
        o_ref[...] += down((j - 1) % 2)
        act_ref[j % 2] = activation(*up)

    @pl.when(j == last)
    def _():
        o_ref[...] += down((j - 1) % 2)


def _ffn(x, g, w_up, conv_w, w_down, seq, tm, tn):
    t, d = x.shape
    nff = D_FF // tn
    halo_per_block = tm // FFN_HALO
    up_blk = lambda j: jnp.minimum(j, nff - 1)
    return pl.pallas_call(
        functools.partial(_ffn_body, tm=tm, blocks_per_seq=seq // tm),
        out_shape=jax.ShapeDtypeStruct((t, d), F32),
        grid=(t // tm, nff + 1),
        in_specs=[pl.BlockSpec((tm, d), lambda i, j: (i, 0)),
                  pl.BlockSpec((FFN_HALO, d), lambda i, j: (jnp.maximum(i * halo_per_block - 1, 0), 0)),
                  pl.BlockSpec((1, d), lambda i, j: (0, 0)),
                  pl.BlockSpec((d, tn), lambda i, j: (0, up_blk(j))),
                  pl.BlockSpec((d, tn), lambda i, j: (0, nff + up_blk(j))),
                  pl.BlockSpec((3, tn), lambda i, j: (0, up_blk(j))),
                  pl.BlockSpec((tn, d), lambda i, j: (jnp.maximum(j - 1, 0), 0))],
        out_specs=pl.BlockSpec((tm, d), lambda i, j: (i, 0)),
        scratch_shapes=[pltpu.VMEM((FFN_HALO + tm, d), BF16), pltpu.VMEM((2, tm, tn), BF16)],
        compiler_params=_cparams(("parallel", "arbitrary")),
        name="conv_glu_ffn",
    )(x, x, g, w_up, w_up, conv_w, w_down)


RW_IN = 3 * A_DIM + LORA_W + LORA_A + LORA_G
B_IN = 3 * B_DIM


def _pad_cols(w, width):
    return jnp.pad(w, ((0, 0), (0, width - w.shape[1])))


def _layout_w_in(w):
    o = 0
    rkv = w[:, o:o + 3 * A_DIM]; o += 3 * A_DIM
    zw = w[:, o:o + LORA_W]; o += LORA_W
    za = w[:, o:o + LORA_A]; o += LORA_A
    zg = w[:, o:o + LORA_G]; o += LORA_G
    qkv = w[:, o:o + B_IN]; o += B_IN
    u = w[:, o:o + C_DIM]; o += C_DIM
    gates = w[:, o:]
    out = jnp.concatenate([gates, rkv, qkv, _pad_cols(zw, LORA_PAD), _pad_cols(za, LORA_PAD), zg, u], axis=1)
    return out.astype(BF16)


def _rwkv_params(mix, w0, w2, a0, a2, g2, k_k, k_a, r_k, ln_w, ln_b):
    row = lambda v: v.reshape(1, -1)
    o = 3 * A_DIM
    mix_l = jnp.concatenate([
        jnp.pad(mix[o:o + LORA_W], (0, LORA_PAD - LORA_W)),
        jnp.pad(mix[o + LORA_W:o + LORA_W + LORA_A], (0, LORA_PAD - LORA_A)),
        mix[o + LORA_W + LORA_A:]])
    pad_rows = lambda m: jnp.pad(m, ((0, LORA_PAD - m.shape[0]), (0, 0)))
    w2 = pad_rows(w2)
    w2h = w2.astype(BF16)
    w2l = (w2 - w2h.astype(F32)).astype(BF16)
    return dict(mix_r=row(mix[0:A_DIM]), mix_k=row(mix[A_DIM:2 * A_DIM]), mix_v=row(mix[2 * A_DIM:3 * A_DIM]),
                mix_l=row(mix_l), w0=row(w0), w2h=w2h, w2l=w2l, a0=row(a0), a2=pad_rows(a2).astype(BF16),
                g2=g2.astype(BF16), k_k=row(k_k), k_a=row(k_a), r_k=row(r_k), ln_w=row(ln_w), ln_b=row(ln_b))


def _s5_params(lam_re, lam_im, log_dt, b_re, b_im, c_re, c_im, d, glu_w, glu_b):
    dt = jnp.exp(log_dt)[:, None]
    er = jnp.exp(lam_re * dt)
    ab_re = er * jnp.cos(lam_im * dt)
    ab_im = er * jnp.sin(lam_im * dt)
    den = lam_re * lam_re + lam_im * lam_im
    nr = ab_re - 1.0
    f_re = (nr * lam_re + ab_im * lam_im) / den
    f_im = (ab_im * lam_re - nr * lam_im) / den
    bb_re = f_re[..., None] * b_re - f_im[..., None] * b_im
    bb_im = f_re[..., None] * b_im + f_im[..., None] * b_re
    eye_g = jnp.eye(C_GROUPS, dtype=F32)
    bd_re = jnp.einsum("gpc,gh->gchp", bb_re, eye_g).reshape(C_DIM, C_NS)
    bd_im = jnp.einsum("gpc,gh->gchp", bb_im, eye_g).reshape(C_DIM, C_NS)
    cd_re = jnp.einsum("gcp,gh->gphc", c_re, eye_g).reshape(C_NS, C_DIM)
    cd_im = jnp.einsum("gcp,gh->gphc", c_im, eye_g).reshape(C_NS, C_DIM)
    pr, pi = [ab_re.reshape(-1)], [ab_im.reshape(-1)]
    for _ in range(S5_SUB - 1):
        pr.append(pr[-1] * pr[0] - pi[-1] * pi[0])
        pi.append(pr[-2] * pi[0] + pi[-1] * pr[0])
    pw = jnp.concatenate([jnp.stack(pr), jnp.stack(pi)], axis=1)
    return dict(bd=jnp.concatenate([bd_re, bd_im], axis=1).astype(BF16),
                cd=jnp.concatenate([cd_re, -cd_im], axis=0).astype(BF16),
                pw=pw, d=d.reshape(1, -1), glu_w=glu_w.astype(BF16), glu_b=glu_b.reshape(1, -1))


def kernel(x, positions, norm_mix, norm_ffn, w_in, b_gate, rw_mix, rw_w0, rw_w2, rw_a0, rw_a2, rw_g2, rw_k_k, rw_k_a, rw_r_k, rw_ln_w, rw_ln_b, da_lq1, da_lk1, da_lq2, da_lk2, da_subln, s5_lam_re, s5_lam_im, s5_log_dt, s5_b_re, s5_b_im, s5_c_re, s5_c_im, s5_d, s5_glu_w, s5_glu_b, proj_a, proj_b, proj_c, w_out, ffn_up, ffn_conv, ffn_down, norm_final):
    bsz, seq, d = x.shape
    t = bsz * seq
    tm_proj = min(1024, seq)
    tm_ffn = min(512, seq)
    tm_small = min(256, seq)
    tq = min(512, seq)
    lt = min(256, seq)

    inv_freq = ROPE_THETA ** (-jnp.arange(0, B_QK, 2, dtype=F32) / B_QK)
    ang = positions.astype(F32)[..., None] * inv_freq
    cos, sin = jnp.cos(ang), jnp.sin(ang)
    cos_t = jnp.concatenate([cos, cos, cos, cos], axis=-1).reshape(t, 128)
    sin_t = jnp.concatenate([-sin, sin, -sin, sin], axis=-1).reshape(t, 128)

    xf = x.reshape(t, d)
    for l in range(DEPTH):
        z = _norm_matmul(xf, norm_mix[l].reshape(1, d), _layout_w_in(w_in[l]), tm_proj, 512)

        rp = _rwkv_params(rw_mix[l], rw_w0[l], rw_w2[l], rw_a0[l], rw_a2[l], rw_g2[l], rw_k_k[l],
                          rw_k_a[l], rw_r_k[l].reshape(-1), rw_ln_w[l], rw_ln_b[l])
        y_a = _rwkv_mixer(z, rp, bsz, seq)

        lam_init = 0.8 - 0.6 * math.exp(-0.3 * l)
        lam = jnp.exp(jnp.sum(da_lq1[l] * da_lk1[l])) - jnp.exp(jnp.sum(da_lq2[l] * da_lk2[l])) + lam_init
        q1, q2, kr, vr = _rope_prep(z, cos_t, sin_t, tm_ffn)
        shp = (bsz, seq, B_DIM)
        y_b = _diff_attention(q1.reshape(shp), q2.reshape(shp), kr.reshape(shp), vr.reshape(shp),
                              jnp.full((1, B_V), lam, F32),
                              (da_subln[l] * (1.0 - lam_init)).reshape(1, B_V), tq).reshape(t, B_DIM)

        sp = _s5_params(s5_lam_re[l], s5_lam_im[l], s5_log_dt[l], s5_b_re[l], s5_b_im[l], s5_c_re[l],
                        s5_c_im[l], s5_d[l], s5_glu_w[l], s5_glu_b[l])
        y_c = _s5_mixer(z, sp, bsz, seq, lt)

        xf = _merge_out(y_a, y_b, y_c, z, b_gate[l].reshape(1, -1), proj_a[l].astype(BF16),
                        proj_b[l].astype(BF16), proj_c[l].astype(BF16), w_out[l].astype(BF16), xf, tm_small)
        xf = _ffn(xf, norm_ffn[l].reshape(1, d), ffn_up[l].astype(BF16), ffn_conv[l],
                  ffn_down[l].astype(BF16), seq, tm_proj, 256)
    return _final_norm(xf, norm_final.reshape(1, d), tm_ffn).reshape(bsz, seq, d)
```

```python
import functools
import math

import jax
import jax.numpy as jnp
import numpy as np
from jax import lax
from jax.experimental import pallas as pl
from jax.experimental.pallas import tpu as pltpu

F32 = jnp.float32
BF16 = jnp.bfloat16

D_MODEL = 2048
DEPTH = 4
CHUNK = 64
EPS = 1e-6
ROPE_THETA = 10000.0

A_HEAD = 64
A_DIM = 768
A_HEADS = 12
LORA_W = 96
LORA_A = 96
LORA_G = 256
LORA_PAD = 128
GN_EPS = 64e-5

B_QK = 64
B_V = 128
B_DIM = 768
B_HEADS = 6

C_GROUP = 16
C_DIM = 512
C_GROUPS = 32
C_STATE = 64
C_NS = C_GROUPS * C_STATE

N_BRANCH = 3
D_FF = 5632

Z_GATE = 0
Z_RKV = N_BRANCH * D_MODEL
Z_QKV = Z_RKV + 3 * A_DIM
Z_LORA = Z_QKV + 3 * B_DIM
LORA_IN = 2 * LORA_PAD + LORA_G
Z_U = Z_LORA + LORA_IN
Z_COLS = Z_U + C_DIM

VMEM_LIMIT = 56 * 1024 * 1024
NEG_BIG = -1e30


def _cparams(sem):
    return pltpu.CompilerParams(dimension_semantics=sem, vmem_limit_bytes=VMEM_LIMIT)


def _gelu_tanh(x):
    return 0.5 * x * (1.0 + jnp.tanh(math.sqrt(2.0 / math.pi) * (x + 0.044715 * (x * x * x))))


def _dot(a, b):
    return jnp.dot(a.astype(BF16), b.astype(BF16), preferred_element_type=F32)


def _dot_f32(a, b):
    return jnp.dot(a, b, preferred_element_type=F32, precision=lax.Precision.HIGHEST)


def _dot_nt(a, b):
    return lax.dot_general(a.astype(BF16), b.astype(BF16), (((1,), (1,)), ((), ())),
                           preferred_element_type=F32)


def _dot_tn(a, b, precision=None):
    return lax.dot_general(a, b, (((0,), (0,)), ((), ())), preferred_element_type=F32,
                           precision=precision)


def _norm_matmul_body(x_ref, g_ref, w_ref, o_ref, h_ref):
    @pl.when(pl.program_id(1) == 0)
    def _():
        x = x_ref[...]
        ms = jnp.mean(x * x, axis=-1, keepdims=True)
        h_ref[...] = (x * lax.rsqrt(ms + EPS) * g_ref[...]).astype(BF16)

    o_ref[...] = jnp.dot(h_ref[...], w_ref[...], preferred_element_type=F32).astype(o_ref.dtype)


def _norm_matmul(x, g, w, tm, tn):
    t, d = x.shape
    n = w.shape[1]
    return pl.pallas_call(
        _norm_matmul_body,
        out_shape=jax.ShapeDtypeStruct((t, n), BF16),
        grid=(t // tm, n // tn),
        in_specs=[pl.BlockSpec((tm, d), lambda i, j: (i, 0)),
                  pl.BlockSpec((1, d), lambda i, j: (0, 0)),
                  pl.BlockSpec((d, tn), lambda i, j: (0, j))],
        out_specs=pl.BlockSpec((tm, tn), lambda i, j: (i, j)),
        scratch_shapes=[pltpu.VMEM((tm, d), BF16)],
        compiler_params=_cparams(("parallel", "arbitrary")),
        name="norm_in_proj",
    )(x, g, w)


def _final_norm_body(x_ref, g_ref, o_ref):
    x = x_ref[...]
    ms = jnp.mean(x * x, axis=-1, keepdims=True)
    o_ref[...] = x * lax.rsqrt(ms + EPS) * g_ref[...]


def _final_norm(x, g, tm):
    t, d = x.shape
    return pl.pallas_call(
        _final_norm_body,
        out_shape=jax.ShapeDtypeStruct((t, d), F32),
        grid=(t // tm,),
        in_specs=[pl.BlockSpec((tm, d), lambda i: (i, 0)),
                  pl.BlockSpec((1, d), lambda i: (0, 0))],
        out_specs=pl.BlockSpec((tm, d), lambda i: (i, 0)),
        compiler_params=_cparams(("parallel",)),
        name="final_norm",
    )(x, g)


RW_LANES = 2 * A_HEAD
RW_ROWS = 2 * CHUNK
RW_PAIRS = A_HEADS // 2
RW_NC = 4
RW_SC = 2
RW_HALO = 16


def _shift_rows(x, prev_row):
    row = lax.broadcasted_iota(jnp.int32, x.shape, 0)
    return jnp.where(row == 0, prev_row, pltpu.roll(x, 1, 0))


def _split_bf16(x, pieces):
    out = []
    for _ in range(pieces - 1):
        h = x.astype(BF16)
        out.append(h)
        x = x - h.astype(F32)
    out.append(x.astype(BF16))
    return out


def _pair_masks():
    lane_r = lax.broadcasted_iota(jnp.int32, (RW_LANES, RW_LANES), 0)
    lane_c = lax.broadcasted_iota(jnp.int32, (RW_LANES, RW_LANES), 1)
    same_head = (lane_r // A_HEAD) == (lane_c // A_HEAD)
    head0 = lax.broadcasted_iota(jnp.int32, (CHUNK, RW_LANES), 1) < A_HEAD
    return same_head, head0


def _stack_heads(x, head0):
    return jnp.concatenate([jnp.where(head0, x, 0.0), jnp.where(head0, 0.0, x)], axis=0)


def _unstack_heads(x):
    return x[0:CHUNK, :] + x[CHUNK:, :]


def _rwkv_prep_body(zr_ref, zk_ref, zv_ref, zl_ref, hr_ref, hk_ref, hv_ref, hl_ref,
                    mr_ref, mk_ref, mv_ref, ml_ref,
                    w0_ref, w2h_ref, w2l_ref, a0_ref, a2_ref, g2_ref, kk_ref, ka_ref, rk_ref,
                    w_ref, u0_ref, rt_ref, y0_ref, lrb_ref, kend_ref, bend_ref, v_ref, gam_ref,
                    g_ref, bonus_ref):
    first = pl.program_id(1) == 0

    def token_shift(z_ref, halo_ref, mix_ref):
        z = z_ref[...].astype(F32)
        prev = jnp.where(first, 0.0, halo_ref[RW_HALO - 1:RW_HALO, :].astype(F32))
        return z + (_shift_rows(z, prev) - z) * mix_ref[...]

    r = token_shift(zr_ref, hr_ref, mr_ref)
    k = token_shift(zk_ref, hk_ref, mk_ref)
    v = token_shift(zv_ref, hv_ref, mv_ref)
    zl = token_shift(zl_ref, hl_ref, ml_ref)
    zw = zl[:, 0:LORA_PAD]
    za = zl[:, LORA_PAD:2 * LORA_PAD]
    zg = zl[:, 2 * LORA_PAD:]

    th_h, th_l = _split_bf16(jnp.tanh(zw), 2)
    w2h = w2h_ref[...]
    pre_w = (w0_ref[...] + jnp.dot(th_h, w2h, preferred_element_type=F32)
             + jnp.dot(th_h, w2l_ref[...], preferred_element_type=F32)
             + jnp.dot(th_l, w2h, preferred_element_type=F32))
    neg = -pre_w
    softplus = jnp.maximum(neg, 0.0) + jnp.log(1.0 + jnp.exp(-jnp.abs(neg)))
    logw = -jnp.exp(-softplus - 0.5)
    a = jax.nn.sigmoid(a0_ref[...] + _dot(za, a2_ref[...]))
    g = _dot(jax.nn.sigmoid(zg), g2_ref[...])
    g_ref[...] = g.astype(g_ref.dtype)
    v_ref[...] = v.astype(v_ref.dtype)

    rows = RW_NC * CHUNK
    ti = lax.broadcasted_iota(jnp.int32, (rows, rows), 0)
    tj = lax.broadcasted_iota(jnp.int32, (rows, rows), 1)
    tri = ((tj <= ti) & ((ti // CHUNK) == (tj // CHUNK))).astype(BF16)
    lg = sum(jnp.dot(tri, piece, preferred_element_type=F32) for piece in _split_bf16(logw, 3))
    rss = [slice(ci * CHUNK, (ci + 1) * CHUNK) for ci in range(RW_NC)]
    lg_last = [lg[rs.stop - 1:rs.stop, :] for rs in rss]
    for ci in range(RW_NC):
        gam_ref[ci] = jnp.exp(lg_last[ci])

    kk_raw = k * kk_ref[...]
    k = k * (1.0 + (a - 1.0) * ka_ref[...])
    e_pos = jnp.exp(lg)
    e_neg = jnp.exp(-lg)
    e_prev = jnp.exp(lg - logw)
    e_end = jnp.concatenate([jnp.exp(last - lg[rs, :]) for last, rs in zip(lg_last, rss)], axis=0)
    rt_ref[...] = (r * e_pos).astype(rt_ref.dtype)
    kend_ref[...] = (k * e_end).astype(kend_ref.dtype)
    k_t = k * e_neg
    rkr = r * k * rk_ref[...]

    same_head, head0 = _pair_masks()
    head_ones = same_head.astype(BF16)
    sr = lax.broadcasted_iota(jnp.int32, (RW_ROWS, RW_ROWS), 0)
    sc = lax.broadcasted_iota(jnp.int32, (RW_ROWS, RW_ROWS), 1)
    same_blk = (sr // CHUNK) == (sc // CHUNK)
    strict = same_blk & ((sr % CHUNK) > (sc % CHUNK))
    incl = same_blk & ((sr % CHUNK) >= (sc % CHUNK))
    eye = (sr == sc).astype(F32)
    level_masks = []
    s = 1
    while s < CHUNK:
        level_masks.append(((sr // (2 * s)) == (sc // (2 * s))) & (((sr // s) % 2) == 1) & (((sc // s) % 2) == 0))
        s *= 2

    at = [(rs, slice(p * RW_LANES, (p + 1) * RW_LANES)) for rs in rss for p in range(RW_PAIRS)]
    chains = range(len(at))
    hsum = [jnp.dot(jnp.concatenate([kk_raw[ix] * kk_raw[ix], rkr[ix]], axis=0).astype(BF16), head_ones,
                    preferred_element_type=F32) for ix in at]
    kk = [kk_raw[ix] / jnp.maximum(jnp.sqrt(hs[0:CHUNK, :]), 1e-12) for ix, hs in zip(at, hsum)]
    kka = [kk[c] * a[at[c]] for c in chains]
    for c in chains:
        bend_ref[at[c]] = (kka[c] * e_end[at[c]]).astype(bend_ref.dtype)
        bonus_ref[at[c]] = (hsum[c][CHUNK:, :] * v[at[c]]).astype(bonus_ref.dtype)

    a_s = [_stack_heads(-kk[c] * e_prev[at[c]], head0).astype(BF16) for c in chains]
    b_s = [_stack_heads(kka[c] * e_neg[at[c]], head0).astype(BF16) for c in chains]
    k_s = [_stack_heads(k_t[ix], head0).astype(BF16) for ix in at]
    r_s = [_stack_heads(r[ix] * e_pos[ix], head0).astype(BF16) for ix in at]
    v_s = [_stack_heads(v[ix], head0).astype(BF16) for ix in at]

    ar_bk = [_dot_nt(jnp.concatenate([a_s[c], r_s[c]], axis=0), jnp.concatenate([b_s[c], k_s[c]], axis=0))
             for c in chains]
    l_ab = [jnp.where(strict, m[0:RW_ROWS, 0:RW_ROWS], 0.0) for m in ar_bk]
    l_ak = [jnp.where(strict, m[0:RW_ROWS, RW_ROWS:], 0.0) for m in ar_bk]
    l_rb = [jnp.where(incl, m[RW_ROWS:, 0:RW_ROWS], 0.0) for m in ar_bk]
    l_rk = [jnp.where(incl, m[RW_ROWS:, RW_ROWS:], 0.0) for m in ar_bk]
    for c in chains:
        lrb_ref[at[c]] = _unstack_heads(l_rb[c]).astype(lrb_ref.dtype)
    yx0 = [_dot(jnp.concatenate([l_rk[c], l_ak[c]], axis=0), v_s[c]) for c in chains]
    for c in chains:
        y0_ref[at[c]] = _unstack_heads(yx0[c][0:RW_ROWS, :])

    t_inv = [eye + jnp.where(level_masks[0], l_ab[c], 0.0) for c in chains]
    for m in level_masks[1:]:
        lt = [_dot(jnp.where(m, l_ab[c], 0.0), t_inv[c]) for c in chains]
        t_inv = [t_inv[c] + _dot(t_inv[c], lt[c]) for c in chains]

    wu = [_dot(t_inv[c], jnp.concatenate([a_s[c], yx0[c][RW_ROWS:, :].astype(BF16)], axis=1)) for c in chains]
    for c in chains:
        w_ref[at[c]] = _unstack_heads(wu[c][:, 0:RW_LANES]).astype(w_ref.dtype)
        u0_ref[at[c]] = _unstack_heads(wu[c][:, RW_LANES:])


def _rwkv_scan_body(w_ref, u0_ref, rt_ref, y0_ref, lrb_ref, kend_ref, bend_ref, v_ref, gam_ref,
                    g_ref, bonus_ref, lnw_ref, lnb_ref, o_ref, s_ref):
    @pl.when(pl.program_id(1) == 0)
    def _():
        s_ref[...] = jnp.zeros_like(s_ref)

    same_head, head0 = _pair_masks()
    head_avg = (same_head.astype(F32) * (1.0 / A_HEAD)).astype(BF16)

    pairs = range(RW_PAIRS)
    sls = [slice(p * RW_LANES, (p + 1) * RW_LANES) for p in pairs]
    s = [s_ref[p] for p in pairs]
    for ci in range(RW_SC):
        rs = slice(ci * CHUNK, (ci + 1) * CHUNK)
        gam = gam_ref[ci]
        ws = [lax.dot_general(jnp.concatenate([w_ref[rs, sl], rt_ref[rs, sl]], axis=0), s[p].astype(BF16),
                              (((1,), (1,)), ((), ())), preferred_element_type=F32)
              for p, sl in zip(pairs, sls)]
        u = [ws[p][0:CHUNK, :] + u0_ref[rs, sls[p]] for p in pairs]
        s_add = [_dot_tn(jnp.concatenate([v_ref[rs, sl], u[p].astype(BF16)], axis=0),
                         jnp.concatenate([kend_ref[rs, sl], bend_ref[rs, sl]], axis=0))
                 for p, sl in zip(pairs, sls)]
        s = [s[p] * gam[:, sls[p]] + jnp.where(same_head, s_add[p], 0.0) for p in pairs]
        y = [ws[p][CHUNK:, :] + y0_ref[rs, sls[p]]
             + jnp.dot(lrb_ref[rs, sls[p]], _stack_heads(u[p], head0).astype(BF16), preferred_element_type=F32)
             for p in pairs]

        y_hl = [_split_bf16(y[p], 2) for p in pairs]
        mu = [jnp.dot(h, head_avg, preferred_element_type=F32) + jnp.dot(l, head_avg, preferred_element_type=F32)
              for h, l in y_hl]
        d = [y[p] - mu[p] for p in pairs]
        var = [jnp.dot((d[p] * d[p]).astype(BF16), head_avg, preferred_element_type=F32) for p in pairs]
        for p, sl in zip(pairs, sls):
            yn = d[p] * lax.rsqrt(var[p] + GN_EPS) * lnw_ref[:, sl] + lnb_ref[:, sl]
            o_ref[rs, sl] = ((yn + bonus_ref[rs, sl]) * g_ref[rs, sl].astype(F32)).astype(o_ref.dtype)
    for p in pairs:
        s_ref[p] = s[p]


def _rwkv_mixer(z, p, bsz, seq):
    nchunk = seq // CHUNK
    t = bsz * seq
    rblk = Z_RKV // A_DIM
    lblk = Z_LORA // LORA_IN
    prows = RW_NC * CHUNK
    nprep = seq // prows
    halo_per_step = prows // RW_HALO

    def zspec(width, blk):
        return pl.BlockSpec((prows, width), lambda b, c: (b * nprep + c, blk))

    def hspec(width, blk):
        return pl.BlockSpec((RW_HALO, width),
                            lambda b, c: (jnp.maximum((b * nprep + c) * halo_per_step - 1, 0), blk))

    full = lambda shape: pl.BlockSpec(shape, lambda b, c: (0, 0))
    ptok = pl.BlockSpec((prows, A_DIM), lambda b, c: (b * nprep + c, 0))
    pgam = pl.BlockSpec((RW_NC, 1, A_DIM), lambda b, c: (b * nprep + c, 0, 0))
    nscan = nchunk // RW_SC
    tok = pl.BlockSpec((RW_SC * CHUNK, A_DIM), lambda b, c: (b * nscan + c, 0))
    gam_spec = pl.BlockSpec((RW_SC, 1, A_DIM), lambda b, c: (b * nscan + c, 0, 0))
    bf = jax.ShapeDtypeStruct((t, A_DIM), BF16)
    f32 = jax.ShapeDtypeStruct((t, A_DIM), F32)
    gam_shape = jax.ShapeDtypeStruct((bsz * nchunk, 1, A_DIM), F32)

    prep = pl.pallas_call(
        _rwkv_prep_body,
        out_shape=(bf, f32, bf, f32, bf, bf, bf, bf, gam_shape, bf, f32),
        grid=(bsz, nprep),
        in_specs=[zspec(A_DIM, rblk), zspec(A_DIM, rblk + 1), zspec(A_DIM, rblk + 2), zspec(LORA_IN, lblk),
                  hspec(A_DIM, rblk), hspec(A_DIM, rblk + 1), hspec(A_DIM, rblk + 2), hspec(LORA_IN, lblk),
                  full((1, A_DIM)), full((1, A_DIM)), full((1, A_DIM)), full((1, LORA_IN)),
                  full((1, A_DIM)), full((LORA_PAD, A_DIM)), full((LORA_PAD, A_DIM)),
                  full((1, A_DIM)), full((LORA_PAD, A_DIM)), full((LORA_G, A_DIM)),
                  full((1, A_DIM)), full((1, A_DIM)), full((1, A_DIM))],
        out_specs=(ptok, ptok, ptok, ptok, ptok, ptok, ptok, ptok, pgam, ptok, ptok),
        compiler_params=_cparams(("parallel", "parallel")),
        name="rwkv7_prep",
    )(z, z, z, z, z, z, z, z,
      p["mix_r"], p["mix_k"], p["mix_v"], p["mix_l"],
      p["w0"], p["w2h"], p["w2l"], p["a0"], p["a2"], p["g2"], p["k_k"], p["k_a"], p["r_k"])

    return pl.pallas_call(
        _rwkv_scan_body,
        out_shape=bf,
        grid=(bsz, nscan),
        in_specs=[tok, tok, tok, tok, tok, tok, tok, tok, gam_spec, tok, tok,
                  full((1, A_DIM)), full((1, A_DIM))],
        out_specs=tok,
        scratch_shapes=[pltpu.VMEM((RW_PAIRS, RW_LANES, RW_LANES), F32)],
        compiler_params=_cparams(("parallel", "arbitrary")),
        name="rwkv7_scan",
    )(*prep, p["ln_w"], p["ln_b"])


def _rope_body(zq_ref, zk_ref, zv_ref, cos_ref, sin_ref, q1_ref, q2_ref, k_ref, v_ref):
    reps = B_DIM // 128
    cos = jnp.concatenate([cos_ref[...]] * reps, axis=1)
    sin = jnp.concatenate([sin_ref[...]] * reps, axis=1)
    lane = lax.broadcasted_iota(jnp.int32, cos.shape, 1)
    first_half = (lane % B_QK) < (B_QK // 2)
    comp0 = (lane % B_V) < B_QK

    def rope(x):
        partner = jnp.where(first_half, pltpu.roll(x, B_DIM - B_QK // 2, 1), pltpu.roll(x, B_QK // 2, 1))
        return x * cos + partner * sin

    q = rope(zq_ref[...].astype(F32)) * (B_QK ** -0.5)
    q1_ref[...] = jnp.where(comp0, q, 0.0).astype(BF16)
    q2_ref[...] = jnp.where(comp0, 0.0, q).astype(BF16)
    k_ref[...] = rope(zk_ref[...].astype(F32)).astype(BF16)
    v_ref[...] = zv_ref[...]


def _rope_prep(z, cos_t, sin_t, tm):
    t = z.shape[0]
    qblk = Z_QKV // B_DIM
    out = jax.ShapeDtypeStruct((t, B_DIM), BF16)
    ospec = pl.BlockSpec((tm, B_DIM), lambda i: (i, 0))
    return pl.pallas_call(
        _rope_body,
        out_shape=(out, out, out, out),
        grid=(t // tm,),
        in_specs=[pl.BlockSpec((tm, B_DIM), lambda i: (i, qblk)),
                  pl.BlockSpec((tm, B_DIM), lambda i: (i, qblk + 1)),
                  pl.BlockSpec((tm, B_DIM), lambda i: (i, qblk + 2)),
                  pl.BlockSpec((tm, 128), lambda i: (i, 0)),
                  pl.BlockSpec((tm, 128), lambda i: (i, 0))],
        out_specs=(ospec, ospec, ospec, ospec),
        compiler_params=_cparams(("parallel",)),
        name="rope_prep",
    )(z, z, z, cos_t, sin_t)


ATT_KV_BLOCKS = 2


def _attn_body(q1_ref, q2_ref, k_ref, v_ref, lam_ref, sub_ref, o_ref,
               m1_ref, a1_ref, m2_ref, a2_ref, *, tq):
    i = pl.program_id(2)

    m1_ref[...] = jnp.full_like(m1_ref, NEG_BIG)
    m2_ref[...] = jnp.full_like(m2_ref, NEG_BIG)
    a1_ref[...] = jnp.zeros_like(a1_ref)
    a2_ref[...] = jnp.zeros_like(a2_ref)
    def update(j, nblk, masked):
        tk = nblk * tq
        rows = pl.ds(pl.multiple_of(j * tq, tq), tk)
        kb = k_ref[0, rows, :]
        vext = jnp.concatenate([v_ref[0, rows, :], jnp.ones((tk, B_V), BF16)], axis=1)
        if masked:
            row = lax.broadcasted_iota(jnp.int32, (tq, tk), 0)
            col = lax.broadcasted_iota(jnp.int32, (tq, tk), 1)
            visible = (col // CHUNK) <= (row // CHUNK)
        comps = ((q1_ref, m1_ref, a1_ref), (q2_ref, m2_ref, a2_ref))
        scores = [lax.dot_general(q_ref[0], kb, (((1,), (1,)), ((), ())), preferred_element_type=F32)
                  for q_ref, _, _ in comps]
        probs, alphas = [], []
        for s, (_, m_ref, _) in zip(scores, comps):
            if masked:
                s = jnp.where(visible, s, NEG_BIG)
            cols = [s[:, c * 128:(c + 1) * 128] for c in range(tk // 128)]
            mx = cols[0]
            for c in cols[1:]:
                mx = jnp.maximum(mx, c)
            m_old = m_ref[...]
            m_new = jnp.maximum(m_old, jnp.max(mx, axis=-1, keepdims=True))
            m_ref[...] = m_new
            alphas.append(jnp.exp(m_old - m_new))
            probs.append(jnp.concatenate([jnp.exp((c - m_new).astype(BF16)) for c in cols], axis=1))
        pvs = [jnp.dot(p, vext, preferred_element_type=F32) for p in probs]
        for pv, alpha, (_, _, a_ref) in zip(pvs, alphas, comps):
            a_ref[...] = jnp.concatenate([alpha, alpha], axis=1) * a_ref[...] + pv

    def body(jj, carry):
        update(ATT_KV_BLOCKS * jj, ATT_KV_BLOCKS, False)
        return carry

    lax.fori_loop(0, i // ATT_KV_BLOCKS, body, 0)

    def rest(r, carry):
        update((i // ATT_KV_BLOCKS) * ATT_KV_BLOCKS + r, 1, False)
        return carry

    lax.fori_loop(0, i % ATT_KV_BLOCKS, rest, 0)
    update(i, 1, True)

    a1 = a1_ref[...]
    a2 = a2_ref[...]
    o = a1[:, :B_V] / a1[:, B_V:] - lam_ref[...] * (a2[:, :B_V] / a2[:, B_V:])
    o = o * lax.rsqrt(jnp.mean(o * o, axis=-1, keepdims=True) + EPS) * sub_ref[...]
    o_ref[0] = o.astype(o_ref.dtype)


def _diff_attention(q1, q2, k, v, lam, sub, tq):
    bsz, seq, _ = q1.shape
    nq = seq // tq
    qspec = pl.BlockSpec((1, tq, B_V), lambda b, h, i: (b, i, h))
    kspec = pl.BlockSpec((1, seq, B_V), lambda b, h, i: (b, 0, h))
    pspec = pl.BlockSpec((1, B_V), lambda b, h, i: (0, 0))
    return pl.pallas_call(
        functools.partial(_attn_body, tq=tq),
        out_shape=jax.ShapeDtypeStruct((bsz, seq, B_DIM), BF16),
        grid=(bsz, B_HEADS, nq),
        in_specs=[qspec, qspec, kspec, kspec, pspec, pspec],
        out_specs=pl.BlockSpec((1, tq, B_V), lambda b, h, i: (b, i, h)),
        scratch_shapes=[pltpu.VMEM((tq, 128), F32), pltpu.VMEM((tq, 2 * B_V), F32),
                        pltpu.VMEM((tq, 128), F32), pltpu.VMEM((tq, 2 * B_V), F32)],
        compiler_params=_cparams(("parallel", "parallel", "arbitrary")),
        name="diff_attention",
    )(q1, q2, k, v, lam, sub)


S5_SUB = 8
S5_STEPS = (1, 2, 4)


def _s5_body(u_ref, bd_ref, cd_ref, pw_ref, lv_ref, d_ref, gw_ref, gb_ref, o_ref, xs_ref, carry_ref, *, lt):
    t = pl.program_id(1)

    @pl.when(t == 0)
    def _():
        carry_ref[...] = jnp.zeros_like(carry_ref)

    u_b = u_ref[...]
    u = u_b.astype(F32)
    bu = jnp.dot(u_b, bd_ref[...], preferred_element_type=F32)
    re = bu[:, :C_NS]
    im = bu[:, C_NS:]

    groups = lt // S5_SUB
    re = re.reshape(groups, S5_SUB, C_NS)
    im = im.reshape(groups, S5_SUB, C_NS)
    for level, d in enumerate(S5_STEPS):
        pr = lv_ref[level, :, :C_NS][None]
        pi = lv_ref[level, :, C_NS:][None]
        sre = pltpu.roll(re, d, 1)
        sim = pltpu.roll(im, d, 1)
        re, im = re + pr * sre - pi * sim, im + pr * sim + pi * sre
    xs_ref[:, :C_NS] = re.reshape(lt, C_NS)
    xs_ref[:, C_NS:] = im.reshape(lt, C_NS)

    pw_re = pw_ref[:, :C_NS]
    pw_im = pw_ref[:, C_NS:]

    def group(gidx, carry):
        cr, ci = carry
        rows = pl.ds(pl.multiple_of(gidx * S5_SUB, S5_SUB), S5_SUB)
        nre = xs_ref[rows, :C_NS] + pw_re * cr - pw_im * ci
        nim = xs_ref[rows, C_NS:] + pw_re * ci + pw_im * cr
        xs_ref[rows, :C_NS] = nre
        xs_ref[rows, C_NS:] = nim
        return nre[S5_SUB - 1:S5_SUB, :], nim[S5_SUB - 1:S5_SUB, :]

    cr, ci = lax.fori_loop(0, lt // S5_SUB, group, (carry_ref[0:1, :C_NS], carry_ref[0:1, C_NS:]))
    carry_ref[0:1, :C_NS] = cr
    carry_ref[0:1, C_NS:] = ci

    y = _dot(xs_ref[...], cd_ref[...]) + d_ref[...] * u
    y = _gelu_tanh(y)
    y = y * jax.nn.sigmoid(_dot(y, gw_ref[...]) + gb_ref[...])
    o_ref[...] = y.astype(o_ref.dtype)


def _s5_mixer(z, p, bsz, seq, lt):
    nt = seq // lt
    ublk = Z_U // C_DIM
    full = lambda shape: pl.BlockSpec(shape, lambda b, t: (0, 0))
    return pl.pallas_call(
        functools.partial(_s5_body, lt=lt),
        out_shape=jax.ShapeDtypeStruct((bsz * seq, C_DIM), BF16),
        grid=(bsz, nt),
        in_specs=[pl.BlockSpec((lt, C_DIM), lambda b, t: (b * nt + t, ublk)),
                  full((C_DIM, 2 * C_NS)), full((2 * C_NS, C_DIM)), full((S5_SUB, 2 * C_NS)),
                  pl.BlockSpec((len(S5_STEPS), S5_SUB, 2 * C_NS), lambda b, t: (0, 0, 0)),
                  full((1, C_DIM)), full((C_DIM, C_DIM)), full((1, C_DIM))],
        out_specs=pl.BlockSpec((lt, C_DIM), lambda b, t: (b * nt + t, 0)),
        scratch_shapes=[pltpu.VMEM((lt, 2 * C_NS), F32), pltpu.VMEM((S5_SUB, 2 * C_NS), F32)],
        compiler_params=_cparams(("parallel", "arbitrary")),
        name="s5_mixer",
    )(z, p["bd"], p["cd"], p["pw"], p["lv"], p["d"], p["glu_w"], p["glu_b"])


def _merge_out_body(ya_ref, yb_ref, yc_ref, g0_ref, g1_ref, g2_ref, bg_ref, pa_ref, pb_ref, pc_ref,
                    wo_ref, x_ref, o_ref):
    bg = bg_ref[...]
    m = jax.nn.sigmoid(g0_ref[...].astype(F32) + bg[:, 0:D_MODEL]) * jnp.dot(
        ya_ref[...], pa_ref[...], preferred_element_type=F32)
    m = m + jax.nn.sigmoid(g1_ref[...].astype(F32) + bg[:, D_MODEL:2 * D_MODEL]) * jnp.dot(
        yb_ref[...], pb_ref[...], preferred_element_type=F32)
    m = m + jax.nn.sigmoid(g2_ref[...].astype(F32) + bg[:, 2 * D_MODEL:]) * jnp.dot(
        yc_ref[...], pc_ref[...], preferred_element_type=F32)
    o_ref[...] = x_ref[...] + jnp.dot(m.astype(BF16), wo_ref[...], preferred_element_type=F32)


def _merge_out(ya, yb, yc, z, bg, pa, pb, pc, wo, x, tm):
    t = ya.shape[0]
    full = lambda shape: pl.BlockSpec(shape, lambda i: (0, 0))
    return pl.pallas_call(
        _merge_out_body,
        out_shape=jax.ShapeDtypeStruct((t, D_MODEL), F32),
        grid=(t // tm,),
        in_specs=[pl.BlockSpec((tm, A_DIM), lambda i: (i, 0)),
                  pl.BlockSpec((tm, B_DIM), lambda i: (i, 0)),
                  pl.BlockSpec((tm, C_DIM), lambda i: (i, 0)),
                  pl.BlockSpec((tm, D_MODEL), lambda i: (i, 0)),
                  pl.BlockSpec((tm, D_MODEL), lambda i: (i, 1)),
                  pl.BlockSpec((tm, D_MODEL), lambda i: (i, 2)),
                  full((1, N_BRANCH * D_MODEL)),
                  full((A_DIM, D_MODEL)), full((B_DIM, D_MODEL)), full((C_DIM, D_MODEL)),
                  full((D_MODEL, D_MODEL)),
                  pl.BlockSpec((tm, D_MODEL), lambda i: (i, 0))],
        out_specs=pl.BlockSpec((tm, D_MODEL), lambda i: (i, 0)),
        compiler_params=_cparams(("parallel",)),
        name="merge_out_proj",
    )(ya, yb, yc, z, z, z, bg, pa, pb, pc, wo, x)


FFN_HALO = 16


def _ffn_body(x_ref, xh_ref, g_ref, wv_ref, wg_ref, cw_ref, wd_ref, o_ref, h_ref, act_ref,
              *, tm, blocks_per_seq):
    i = pl.program_id(0)
    j = pl.program_id(1)
    last = pl.num_programs(1) - 1

    def norm(x):
        ms = jnp.mean(x * x, axis=-1, keepdims=True)
        return x * lax.rsqrt(ms + EPS) * g_ref[...]

    def up_gate():
        val = jnp.dot(h_ref[FFN_HALO:, :], wv_ref[...], preferred_element_type=F32)
        gate = jnp.dot(h_ref[...], wg_ref[...], preferred_element_type=F32)
        return val, gate

    def activation(val, gate_ext):
        gate = gate_ext[FFN_HALO:, :]
        row = lax.broadcasted_iota(jnp.int32, gate.shape, 0)
        prev1 = gate_ext[FFN_HALO - 1:FFN_HALO, :]
        prev2 = gate_ext[FFN_HALO - 2:FFN_HALO - 1, :]
        gm1 = jnp.where(row == 0, prev1, pltpu.roll(gate, 1, 0))
        gm2 = jnp.where(row == 0, prev2, jnp.where(row == 1, prev1, pltpu.roll(gate, 2, 0)))
        cw = cw_ref[...]
        conv = cw[0:1, :] * gm2 + cw[1:2, :] * gm1 + cw[2:3, :] * gate
        return (_gelu_tanh(conv) * val).astype(BF16)

    def down(slot):
        return jnp.dot(act_ref[slot], wd_ref[...], preferred_element_type=F32)

    @pl.when(j == 0)
    def _():
        h_ref[FFN_HALO:, :] = norm(x_ref[...]).astype(BF16)
        seq_start = (i % blocks_per_seq) == 0
        h_ref[0:FFN_HALO, :] = jnp.where(seq_start, 0.0, norm(xh_ref[...])).astype(BF16)
        o_ref[...] = x_ref[...]
        act_ref[0] = activation(*up_gate())

    @pl.when((j > 0) & (j < last))
    def _():
        up = up_gate()
        o_ref[...] += down((j - 1) % 2)
        act_ref[j % 2] = activation(*up)

    @pl.when(j == last)
    def _():
        o_ref[...] += down((j - 1) % 2)


def _ffn(x, g, w_up, conv_w, w_down, seq, tm, tn):
    t, d = x.shape
    nff = D_FF // tn
    halo_per_block = tm // FFN_HALO
    up_blk = lambda j: jnp.minimum(j, nff - 1)
    return pl.pallas_call(
        functools.partial(_ffn_body, tm=tm, blocks_per_seq=seq // tm),
        out_shape=jax.ShapeDtypeStruct((t, d), F32),
        grid=(t // tm, nff + 1),
        in_specs=[pl.BlockSpec((tm, d), lambda i, j: (i, 0)),
                  pl.BlockSpec((FFN_HALO, d), lambda i, j: (jnp.maximum(i * halo_per_block - 1, 0), 0)),
                  pl.BlockSpec((1, d), lambda i, j: (0, 0)),
                  pl.BlockSpec((d, tn), lambda i, j: (0, up_blk(j))),
                  pl.BlockSpec((d, tn), lambda i, j: (0, nff + up_blk(j))),
                  pl.BlockSpec((3, tn), lambda i, j: (0, up_blk(j))),
                  pl.BlockSpec((tn, d), lambda i, j: (jnp.maximum(j - 1, 0), 0))],
        out_specs=pl.BlockSpec((tm, d), lambda i, j: (i, 0)),
        scratch_shapes=[pltpu.VMEM((FFN_HALO + tm, d), BF16), pltpu.VMEM((2, tm, tn), BF16)],
        compiler_params=_cparams(("parallel", "arbitrary")),
        name="conv_glu_ffn",
    )(x, x, g, w_up, w_up, conv_w, w_down)


RW_IN = 3 * A_DIM + LORA_W + LORA_A + LORA_G
B_IN = 3 * B_DIM


def _pad_cols(w, width):
    return jnp.pad(w, ((0, 0), (0, width - w.shape[1])))


def _layout_w_in(w):
    o = 0
    rkv = w[:, o:o + 3 * A_DIM]; o += 3 * A_DIM
    zw = w[:, o:o + LORA_W]; o += LORA_W
    za = w[:, o:o + LORA_A]; o += LORA_A
    zg = w[:, o:o + LORA_G]; o += LORA_G
    qkv = w[:, o:o + B_IN]; o += B_IN
    u = w[:, o:o + C_DIM]; o += C_DIM
    gates = w[:, o:]
    out = jnp.concatenate([gates, rkv, qkv, _pad_cols(zw, LORA_PAD), _pad_cols(za, LORA_PAD), zg, u], axis=1)
    return out.astype(BF16)


def _rwkv_params(mix, w0, w2, a0, a2, g2, k_k, k_a, r_k, ln_w, ln_b):
    row = lambda v: v.reshape(1, -1)
    o = 3 * A_DIM
    mix_l = jnp.concatenate([
        jnp.pad(mix[o:o + LORA_W], (0, LORA_PAD - LORA_W)),
        jnp.pad(mix[o + LORA_W:o + LORA_W + LORA_A], (0, LORA_PAD - LORA_A)),
        mix[o + LORA_W + LORA_A:]])
    pad_rows = lambda m: jnp.pad(m, ((0, LORA_PAD - m.shape[0]), (0, 0)))
    w2 = pad_rows(w2)
    w2h = w2.astype(BF16)
    w2l = (w2 - w2h.astype(F32)).astype(BF16)
    return dict(mix_r=row(mix[0:A_DIM]), mix_k=row(mix[A_DIM:2 * A_DIM]), mix_v=row(mix[2 * A_DIM:3 * A_DIM]),
                mix_l=row(mix_l), w0=row(w0), w2h=w2h, w2l=w2l, a0=row(a0), a2=pad_rows(a2).astype(BF16),
                g2=g2.astype(BF16), k_k=row(k_k), k_a=row(k_a), r_k=row(r_k), ln_w=row(ln_w), ln_b=row(ln_b))


def _s5_params(lam_re, lam_im, log_dt, b_re, b_im, c_re, c_im, d, glu_w, glu_b):
    dt = jnp.exp(log_dt)[:, None]
    er = jnp.exp(lam_re * dt)
    ab_re = er * jnp.cos(lam_im * dt)
    ab_im = er * jnp.sin(lam_im * dt)
    den = lam_re * lam_re + lam_im * lam_im
    nr = ab_re - 1.0
    f_re = (nr * lam_re + ab_im * lam_im) / den
    f_im = (ab_im * lam_re - nr * lam_im) / den
    bb_re = f_re[..., None] * b_re - f_im[..., None] * b_im
    bb_im = f_re[..., None] * b_im + f_im[..., None] * b_re
    eye_g = jnp.eye(C_GROUPS, dtype=F32)
    bd_re = jnp.einsum("gpc,gh->gchp", bb_re, eye_g).reshape(C_DIM, C_NS)
    bd_im = jnp.einsum("gpc,gh->gchp", bb_im, eye_g).reshape(C_DIM, C_NS)
    cd_re = jnp.einsum("gcp,gh->gphc", c_re, eye_g).reshape(C_NS, C_DIM)
    cd_im = jnp.einsum("gcp,gh->gphc", c_im, eye_g).reshape(C_NS, C_DIM)
    pr, pi = [ab_re.reshape(-1)], [ab_im.reshape(-1)]
    for _ in range(S5_SUB - 1):
        pr.append(pr[-1] * pr[0] - pi[-1] * pi[0])
        pi.append(pr[-2] * pi[0] + pi[-1] * pr[0])
    pw = jnp.concatenate([jnp.stack(pr), jnp.stack(pi)], axis=1)
    sub = jnp.arange(S5_SUB)[:, None]
    lv = jnp.stack([jnp.where(sub >= step, pw[step - 1][None, :], 0.0) for step in S5_STEPS])
    return dict(bd=jnp.concatenate([bd_re, bd_im], axis=1).astype(BF16),
                cd=jnp.concatenate([cd_re, -cd_im], axis=0).astype(BF16),
                pw=pw, lv=lv, d=d.reshape(1, -1), glu_w=glu_w.astype(BF16), glu_b=glu_b.reshape(1, -1))


def kernel(x, positions, norm_mix, norm_ffn, w_in, b_gate, rw_mix, rw_w0, rw_w2, rw_a0, rw_a2, rw_g2, rw_k_k, rw_k_a, rw_r_k, rw_ln_w, rw_ln_b, da_lq1, da_lk1, da_lq2, da_lk2, da_subln, s5_lam_re, s5_lam_im, s5_log_dt, s5_b_re, s5_b_im, s5_c_re, s5_c_im, s5_d, s5_glu_w, s5_glu_b, proj_a, proj_b, proj_c, w_out, ffn_up, ffn_conv, ffn_down, norm_final):
    bsz, seq, d = x.shape
    t = bsz * seq
    tm_proj = min(1024, seq)
    tm_ffn = min(512, seq)
    tm_small = min(256, seq)
    tq = min(512, seq)
    lt = min(256, seq)

    inv_freq = ROPE_THETA ** (-jnp.arange(0, B_QK, 2, dtype=F32) / B_QK)
    ang = positions.astype(F32)[..., None] * inv_freq
    cos, sin = jnp.cos(ang), jnp.sin(ang)
    cos_t = jnp.concatenate([cos, cos, cos, cos], axis=-1).reshape(t, 128)
    sin_t = jnp.concatenate([-sin, sin, -sin, sin], axis=-1).reshape(t, 128)

    xf = x.reshape(t, d)
    for l in range(DEPTH):
        z = _norm_matmul(xf, norm_mix[l].reshape(1, d), _layout_w_in(w_in[l]), tm_proj, 512)

        rp = _rwkv_params(rw_mix[l], rw_w0[l], rw_w2[l], rw_a0[l], rw_a2[l], rw_g2[l], rw_k_k[l],
                          rw_k_a[l], rw_r_k[l].reshape(-1), rw_ln_w[l], rw_ln_b[l])
        y_a = _rwkv_mixer(z, rp, bsz, seq)

        lam_init = 0.8 - 0.6 * math.exp(-0.3 * l)
        lam = jnp.exp(jnp.sum(da_lq1[l] * da_lk1[l])) - jnp.exp(jnp.sum(da_lq2[l] * da_lk2[l])) + lam_init
        q1, q2, kr, vr = _rope_prep(z, cos_t, sin_t, tm_ffn)
        shp = (bsz, seq, B_DIM)
        y_b = _diff_attention(q1.reshape(shp), q2.reshape(shp), kr.reshape(shp), vr.reshape(shp),
                              jnp.full((1, B_V), lam, F32),
                              (da_subln[l] * (1.0 - lam_init)).reshape(1, B_V), tq).reshape(t, B_DIM)

        sp = _s5_params(s5_lam_re[l], s5_lam_im[l], s5_log_dt[l], s5_b_re[l], s5_b_im[l], s5_c_re[l],
                        s5_c_im[l], s5_d[l], s5_glu_w[l], s5_glu_b[l])
        y_c = _s5_mixer(z, sp, bsz, seq, lt)

        xf = _merge_out(y_a, y_b, y_c, z, b_gate[l].reshape(1, -1), proj_a[l].astype(BF16),
                        proj_b[l].astype(BF16), proj_c[l].astype(BF16), w_out[l].astype(BF16), xf, tm_small)
        xf = _ffn(xf, norm_ffn[l].reshape(1, d), ffn_up[l].astype(BF16), ffn_conv[l],
                  ffn_down[l].astype(BF16), seq, tm_proj, 256)
    return _final_norm(xf, norm_final.reshape(1, d), tm_ffn).reshape(bsz, seq, d)
```

```python
import functools
import math

import jax
import jax.numpy as jnp
import numpy as np
from jax import lax
from jax.experimental import pallas as pl
from jax.experimental.pallas import tpu as pltpu

F32 = jnp.float32
BF16 = jnp.bfloat16

D_MODEL = 2048
DEPTH = 4
CHUNK = 64
EPS = 1e-6
ROPE_THETA = 10000.0

A_HEAD = 64
A_DIM = 768
A_HEADS = 12
LORA_W = 96
LORA_A = 96
LORA_G = 256
LORA_PAD = 128
GN_EPS = 64e-5

B_QK = 64
B_V = 128
B_DIM = 768
B_HEADS = 6

C_GROUP = 16
C_DIM = 512
C_GROUPS = 32
C_STATE = 64
C_NS = C_GROUPS * C_STATE

N_BRANCH = 3
D_FF = 5632

Z_GATE = 0
Z_RKV = N_BRANCH * D_MODEL
Z_QKV = Z_RKV + 3 * A_DIM
Z_LORA = Z_QKV + 3 * B_DIM
LORA_IN = 2 * LORA_PAD + LORA_G
Z_U = Z_LORA + LORA_IN
Z_COLS = Z_U + C_DIM

VMEM_LIMIT = 56 * 1024 * 1024
NEG_BIG = -1e30


def _cparams(sem):
    return pltpu.CompilerParams(dimension_semantics=sem, vmem_limit_bytes=VMEM_LIMIT)


def _gelu_tanh(x):
    return 0.5 * x * (1.0 + jnp.tanh(math.sqrt(2.0 / math.pi) * (x + 0.044715 * (x * x * x))))


def _dot(a, b):
    return jnp.dot(a.astype(BF16), b.astype(BF16), preferred_element_type=F32)


def _dot_f32(a, b):
    return jnp.dot(a, b, preferred_element_type=F32, precision=lax.Precision.HIGHEST)


def _dot_nt(a, b):
    return lax.dot_general(a.astype(BF16), b.astype(BF16), (((1,), (1,)), ((), ())),
                           preferred_element_type=F32)


def _dot_tn(a, b, precision=None):
    return lax.dot_general(a, b, (((0,), (0,)), ((), ())), preferred_element_type=F32,
                           precision=precision)


def _norm_matmul_body(x_ref, g_ref, w_ref, o_ref, h_ref):
    @pl.when(pl.program_id(1) == 0)
    def _():
        x = x_ref[...]
        ms = jnp.mean(x * x, axis=-1, keepdims=True)
        h_ref[...] = (x * lax.rsqrt(ms + EPS) * g_ref[...]).astype(BF16)

    o_ref[...] = jnp.dot(h_ref[...], w_ref[...], preferred_element_type=F32).astype(o_ref.dtype)


def _norm_matmul(x, g, w, tm, tn):
    t, d = x.shape
    n = w.shape[1]
    return pl.pallas_call(
        _norm_matmul_body,
        out_shape=jax.ShapeDtypeStruct((t, n), BF16),
        grid=(t // tm, n // tn),
        in_specs=[pl.BlockSpec((tm, d), lambda i, j: (i, 0)),
                  pl.BlockSpec((1, d), lambda i, j: (0, 0)),
                  pl.BlockSpec((d, tn), lambda i, j: (0, j))],
        out_specs=pl.BlockSpec((tm, tn), lambda i, j: (i, j)),
        scratch_shapes=[pltpu.VMEM((tm, d), BF16)],
        compiler_params=_cparams(("parallel", "arbitrary")),
        name="norm_in_proj",
    )(x, g, w)


def _final_norm_body(x_ref, g_ref, o_ref):
    x = x_ref[...]
    ms = jnp.mean(x * x, axis=-1, keepdims=True)
    o_ref[...] = x * lax.rsqrt(ms + EPS) * g_ref[...]


def _final_norm(x, g, tm):
    t, d = x.shape
    return pl.pallas_call(
        _final_norm_body,
        out_shape=jax.ShapeDtypeStruct((t, d), F32),
        grid=(t // tm,),
        in_specs=[pl.BlockSpec((tm, d), lambda i: (i, 0)),
                  pl.BlockSpec((1, d), lambda i: (0, 0))],
        out_specs=pl.BlockSpec((tm, d), lambda i: (i, 0)),
        compiler_params=_cparams(("parallel",)),
        name="final_norm",
    )(x, g)


RW_LANES = 2 * A_HEAD
RW_ROWS = 2 * CHUNK
RW_PAIRS = A_HEADS // 2
RW_NC = 4
RW_SC = 2
RW_HALO = 16


def _shift_rows(x, prev_row):
    row = lax.broadcasted_iota(jnp.int32, x.shape, 0)
    return jnp.where(row == 0, prev_row, pltpu.roll(x, 1, 0))


def _split_bf16(x, pieces):
    out = []
    for _ in range(pieces - 1):
        h = x.astype(BF16)
        out.append(h)
        x = x - h.astype(F32)
    out.append(x.astype(BF16))
    return out


def _pair_masks():
    lane_r = lax.broadcasted_iota(jnp.int32, (RW_LANES, RW_LANES), 0)
    lane_c = lax.broadcasted_iota(jnp.int32, (RW_LANES, RW_LANES), 1)
    same_head = (lane_r // A_HEAD) == (lane_c // A_HEAD)
    head0 = lax.broadcasted_iota(jnp.int32, (CHUNK, RW_LANES), 1) < A_HEAD
    return same_head, head0


def _stack_heads(x, head0):
    return jnp.concatenate([jnp.where(head0, x, 0.0), jnp.where(head0, 0.0, x)], axis=0)


def _unstack_heads(x):
    return x[0:CHUNK, :] + x[CHUNK:, :]


def _rwkv_prep_body(zr_ref, zk_ref, zv_ref, zl_ref, hr_ref, hk_ref, hv_ref, hl_ref,
                    mr_ref, mk_ref, mv_ref, ml_ref,
                    w0_ref, w2h_ref, w2l_ref, a0_ref, a2_ref, g2_ref, kk_ref, ka_ref, rk_ref,
                    w_ref, u0_ref, rt_ref, y0_ref, lrb_ref, kend_ref, bend_ref, v_ref, gam_ref,
                    g_ref, bonus_ref):
    first = pl.program_id(1) == 0

    def token_shift(z_ref, halo_ref, mix_ref):
        z = z_ref[...].astype(F32)
        prev = jnp.where(first, 0.0, halo_ref[RW_HALO - 1:RW_HALO, :].astype(F32))
        return z + (_shift_rows(z, prev) - z) * mix_ref[...]

    r = token_shift(zr_ref, hr_ref, mr_ref)
    k = token_shift(zk_ref, hk_ref, mk_ref)
    v = token_shift(zv_ref, hv_ref, mv_ref)
    zl = token_shift(zl_ref, hl_ref, ml_ref)
    zw = zl[:, 0:LORA_PAD]
    za = zl[:, LORA_PAD:2 * LORA_PAD]
    zg = zl[:, 2 * LORA_PAD:]

    th_h, th_l = _split_bf16(jnp.tanh(zw), 2)
    w2h = w2h_ref[...]
    pre_w = (w0_ref[...] + jnp.dot(th_h, w2h, preferred_element_type=F32)
             + jnp.dot(th_h, w2l_ref[...], preferred_element_type=F32)
             + jnp.dot(th_l, w2h, preferred_element_type=F32))
    neg = -pre_w
    softplus = jnp.maximum(neg, 0.0) + jnp.log(1.0 + jnp.exp(-jnp.abs(neg)))
    logw = -jnp.exp(-softplus - 0.5)
    a = jax.nn.sigmoid(a0_ref[...] + _dot(za, a2_ref[...]))
    g = _dot(jax.nn.sigmoid(zg), g2_ref[...])
    g_ref[...] = g.astype(g_ref.dtype)
    v_ref[...] = v.astype(v_ref.dtype)

    rows = RW_NC * CHUNK
    ti = lax.broadcasted_iota(jnp.int32, (rows, rows), 0)
    tj = lax.broadcasted_iota(jnp.int32, (rows, rows), 1)
    tri = ((tj <= ti) & ((ti // CHUNK) == (tj // CHUNK))).astype(BF16)
    lg = sum(jnp.dot(tri, piece, preferred_element_type=F32) for piece in _split_bf16(logw, 3))
    rss = [slice(ci * CHUNK, (ci + 1) * CHUNK) for ci in range(RW_NC)]
    lg_last = [lg[rs.stop - 1:rs.stop, :] for rs in rss]
    for ci in range(RW_NC):
        gam_ref[ci] = jnp.exp(lg_last[ci])

    kk_raw = k * kk_ref[...]
    k = k * (1.0 + (a - 1.0) * ka_ref[...])
    e_pos = jnp.exp(lg)
    e_neg = jnp.exp(-lg)
    e_prev = jnp.exp(lg - logw)
    e_end = jnp.concatenate([jnp.exp(last - lg[rs, :]) for last, rs in zip(lg_last, rss)], axis=0)
    rt_ref[...] = (r * e_pos).astype(rt_ref.dtype)
    kend_ref[...] = (k * e_end).astype(kend_ref.dtype)
    k_t = k * e_neg
    rkr = r * k * rk_ref[...]

    same_head, head0 = _pair_masks()
    head_ones = same_head.astype(BF16)
    sr = lax.broadcasted_iota(jnp.int32, (RW_ROWS, RW_ROWS), 0)
    sc = lax.broadcasted_iota(jnp.int32, (RW_ROWS, RW_ROWS), 1)
    same_blk = (sr // CHUNK) == (sc // CHUNK)
    strict = same_blk & ((sr % CHUNK) > (sc % CHUNK))
    incl = same_blk & ((sr % CHUNK) >= (sc % CHUNK))
    eye = (sr == sc).astype(F32)
    level_masks = []
    s = 1
    while s < CHUNK:
        level_masks.append(((sr // (2 * s)) == (sc // (2 * s))) & (((sr // s) % 2) == 1) & (((sc // s) % 2) == 0))
        s *= 2

    at = [(rs, slice(p * RW_LANES, (p + 1) * RW_LANES)) for rs in rss for p in range(RW_PAIRS)]
    chains = range(len(at))
    hsum = [jnp.dot(jnp.concatenate([kk_raw[ix] * kk_raw[ix], rkr[ix]], axis=0).astype(BF16), head_ones,
                    preferred_element_type=F32) for ix in at]
    kk = [kk_raw[ix] / jnp.maximum(jnp.sqrt(hs[0:CHUNK, :]), 1e-12) for ix, hs in zip(at, hsum)]
    kka = [kk[c] * a[at[c]] for c in chains]
    for c in chains:
        bend_ref[at[c]] = (kka[c] * e_end[at[c]]).astype(bend_ref.dtype)
        bonus_ref[at[c]] = (hsum[c][CHUNK:, :] * v[at[c]]).astype(bonus_ref.dtype)

    a_s = [_stack_heads(-kk[c] * e_prev[at[c]], head0).astype(BF16) for c in chains]
    b_s = [_stack_heads(kka[c] * e_neg[at[c]], head0).astype(BF16) for c in chains]
    k_s = [_stack_heads(k_t[ix], head0).astype(BF16) for ix in at]
    r_s = [_stack_heads(r[ix] * e_pos[ix], head0).astype(BF16) for ix in at]
    v_s = [_stack_heads(v[ix], head0).astype(BF16) for ix in at]

    ar_bk = [_dot_nt(jnp.concatenate([a_s[c], r_s[c]], axis=0), jnp.concatenate([b_s[c], k_s[c]], axis=0))
             for c in chains]
    l_ab = [jnp.where(strict, m[0:RW_ROWS, 0:RW_ROWS], 0.0) for m in ar_bk]
    l_ak = [jnp.where(strict, m[0:RW_ROWS, RW_ROWS:], 0.0) for m in ar_bk]
    l_rb = [jnp.where(incl, m[RW_ROWS:, 0:RW_ROWS], 0.0) for m in ar_bk]
    l_rk = [jnp.where(incl, m[RW_ROWS:, RW_ROWS:], 0.0) for m in ar_bk]
    for c in chains:
        lrb_ref[at[c]] = _unstack_heads(l_rb[c]).astype(lrb_ref.dtype)
    yx0 = [_dot(jnp.concatenate([l_rk[c], l_ak[c]], axis=0), v_s[c]) for c in chains]
    for c in chains:
        y0_ref[at[c]] = _unstack_heads(yx0[c][0:RW_ROWS, :])

    t_inv = [eye + jnp.where(level_masks[0], l_ab[c], 0.0) for c in chains]
    for m in level_masks[1:]:
        lt = [_dot(jnp.where(m, l_ab[c], 0.0), t_inv[c]) for c in chains]
        t_inv = [t_inv[c] + _dot(t_inv[c], lt[c]) for c in chains]

    wu = [_dot(t_inv[c], jnp.concatenate([a_s[c], yx0[c][RW_ROWS:, :].astype(BF16)], axis=1)) for c in chains]
    for c in chains:
        w_ref[at[c]] = _unstack_heads(wu[c][:, 0:RW_LANES]).astype(w_ref.dtype)
        u0_ref[at[c]] = _unstack_heads(wu[c][:, RW_LANES:])


def _rwkv_scan_body(w_ref, u0_ref, rt_ref, y0_ref, lrb_ref, kend_ref, bend_ref, v_ref, gam_ref,
                    g_ref, bonus_ref, lnw_ref, lnb_ref, o_ref, s_ref):
    @pl.when(pl.program_id(1) == 0)
    def _():
        s_ref[...] = jnp.zeros_like(s_ref)

    same_head, head0 = _pair_masks()
    head_avg = (same_head.astype(F32) * (1.0 / A_HEAD)).astype(BF16)

    pairs = range(RW_PAIRS)
    sls = [slice(p * RW_LANES, (p + 1) * RW_LANES) for p in pairs]
    s = [s_ref[p] for p in pairs]
    for ci in range(RW_SC):
        rs = slice(ci * CHUNK, (ci + 1) * CHUNK)
        gam = gam_ref[ci]
        ws = [lax.dot_general(jnp.concatenate([w_ref[rs, sl], rt_ref[rs, sl]], axis=0), s[p].astype(BF16),
                              (((1,), (1,)), ((), ())), preferred_element_type=F32)
              for p, sl in zip(pairs, sls)]
        u = [ws[p][0:CHUNK, :] + u0_ref[rs, sls[p]] for p in pairs]
        s_add = [_dot_tn(jnp.concatenate([v_ref[rs, sl], u[p].astype(BF16)], axis=0),
                         jnp.concatenate([kend_ref[rs, sl], bend_ref[rs, sl]], axis=0))
                 for p, sl in zip(pairs, sls)]
        s = [s[p] * gam[:, sls[p]] + jnp.where(same_head, s_add[p], 0.0) for p in pairs]
        y = [ws[p][CHUNK:, :] + y0_ref[rs, sls[p]]
             + jnp.dot(lrb_ref[rs, sls[p]], _stack_heads(u[p], head0).astype(BF16), preferred_element_type=F32)
             for p in pairs]

        y_hl = [_split_bf16(y[p], 2) for p in pairs]
        mu = [jnp.dot(h, head_avg, preferred_element_type=F32) + jnp.dot(l, head_avg, preferred_element_type=F32)
              for h, l in y_hl]
        d = [y[p] - mu[p] for p in pairs]
        var = [jnp.dot((d[p] * d[p]).astype(BF16), head_avg, preferred_element_type=F32) for p in pairs]
        for p, sl in zip(pairs, sls):
            yn = d[p] * lax.rsqrt(var[p] + GN_EPS) * lnw_ref[:, sl] + lnb_ref[:, sl]
            o_ref[rs, sl] = ((yn + bonus_ref[rs, sl]) * g_ref[rs, sl].astype(F32)).astype(o_ref.dtype)
    for p in pairs:
        s_ref[p] = s[p]


def _rwkv_mixer(z, p, bsz, seq):
    nchunk = seq // CHUNK
    t = bsz * seq
    rblk = Z_RKV // A_DIM
    lblk = Z_LORA // LORA_IN
    prows = RW_NC * CHUNK
    nprep = seq // prows
    halo_per_step = prows // RW_HALO

    def zspec(width, blk):
        return pl.BlockSpec((prows, width), lambda b, c: (b * nprep + c, blk))

    def hspec(width, blk):
        return pl.BlockSpec((RW_HALO, width),
                            lambda b, c: (jnp.maximum((b * nprep + c) * halo_per_step - 1, 0), blk))

    full = lambda shape: pl.BlockSpec(shape, lambda b, c: (0, 0))
    ptok = pl.BlockSpec((prows, A_DIM), lambda b, c: (b * nprep + c, 0))
    pgam = pl.BlockSpec((RW_NC, 1, A_DIM), lambda b, c: (b * nprep + c, 0, 0))
    nscan = nchunk // RW_SC
    tok = pl.BlockSpec((RW_SC * CHUNK, A_DIM), lambda b, c: (b * nscan + c, 0))
    gam_spec = pl.BlockSpec((RW_SC, 1, A_DIM), lambda b, c: (b * nscan + c, 0, 0))
    bf = jax.ShapeDtypeStruct((t, A_DIM), BF16)
    f32 = jax.ShapeDtypeStruct((t, A_DIM), F32)
    gam_shape = jax.ShapeDtypeStruct((bsz * nchunk, 1, A_DIM), F32)

    prep = pl.pallas_call(
        _rwkv_prep_body,
        out_shape=(bf, f32, bf, f32, bf, bf, bf, bf, gam_shape, bf, f32),
        grid=(bsz, nprep),
        in_specs=[zspec(A_DIM, rblk), zspec(A_DIM, rblk + 1), zspec(A_DIM, rblk + 2), zspec(LORA_IN, lblk),
                  hspec(A_DIM, rblk), hspec(A_DIM, rblk + 1), hspec(A_DIM, rblk + 2), hspec(LORA_IN, lblk),
                  full((1, A_DIM)), full((1, A_DIM)), full((1, A_DIM)), full((1, LORA_IN)),
                  full((1, A_DIM)), full((LORA_PAD, A_DIM)), full((LORA_PAD, A_DIM)),
                  full((1, A_DIM)), full((LORA_PAD, A_DIM)), full((LORA_G, A_DIM)),
                  full((1, A_DIM)), full((1, A_DIM)), full((1, A_DIM))],
        out_specs=(ptok, ptok, ptok, ptok, ptok, ptok, ptok, ptok, pgam, ptok, ptok),
        compiler_params=_cparams(("parallel", "parallel")),
        name="rwkv7_prep",
    )(z, z, z, z, z, z, z, z,
      p["mix_r"], p["mix_k"], p["mix_v"], p["mix_l"],
      p["w0"], p["w2h"], p["w2l"], p["a0"], p["a2"], p["g2"], p["k_k"], p["k_a"], p["r_k"])

    return pl.pallas_call(
        _rwkv_scan_body,
        out_shape=bf,
        grid=(bsz, nscan),
        in_specs=[tok, tok, tok, tok, tok, tok, tok, tok, gam_spec, tok, tok,
                  full((1, A_DIM)), full((1, A_DIM))],
        out_specs=tok,
        scratch_shapes=[pltpu.VMEM((RW_PAIRS, RW_LANES, RW_LANES), F32)],
        compiler_params=_cparams(("parallel", "arbitrary")),
        name="rwkv7_scan",
    )(*prep, p["ln_w"], p["ln_b"])


def _rope_body(zq_ref, zk_ref, zv_ref, cos_ref, sin_ref, q1_ref, q2_ref, k_ref, v_ref):
    reps = B_DIM // 128
    cos = jnp.concatenate([cos_ref[...]] * reps, axis=1)
    sin = jnp.concatenate([sin_ref[...]] * reps, axis=1)
    lane = lax.broadcasted_iota(jnp.int32, cos.shape, 1)
    first_half = (lane % B_QK) < (B_QK // 2)
    comp0 = (lane % B_V) < B_QK

    def rope(x):
        partner = jnp.where(first_half, pltpu.roll(x, B_DIM - B_QK // 2, 1), pltpu.roll(x, B_QK // 2, 1))
        return x * cos + partner * sin

    q = rope(zq_ref[...].astype(F32)) * (B_QK ** -0.5)
    q1_ref[...] = jnp.where(comp0, q, 0.0).astype(BF16)
    q2_ref[...] = jnp.where(comp0, 0.0, q).astype(BF16)
    k_ref[...] = rope(zk_ref[...].astype(F32)).astype(BF16)
    v_ref[...] = zv_ref[...]


def _rope_prep(z, cos_t, sin_t, tm):
    t = z.shape[0]
    qblk = Z_QKV // B_DIM
    out = jax.ShapeDtypeStruct((t, B_DIM), BF16)
    ospec = pl.BlockSpec((tm, B_DIM), lambda i: (i, 0))
    return pl.pallas_call(
        _rope_body,
        out_shape=(out, out, out, out),
        grid=(t // tm,),
        in_specs=[pl.BlockSpec((tm, B_DIM), lambda i: (i, qblk)),
                  pl.BlockSpec((tm, B_DIM), lambda i: (i, qblk + 1)),
                  pl.BlockSpec((tm, B_DIM), lambda i: (i, qblk + 2)),
                  pl.BlockSpec((tm, 128), lambda i: (i, 0)),
                  pl.BlockSpec((tm, 128), lambda i: (i, 0))],
        out_specs=(ospec, ospec, ospec, ospec),
        compiler_params=_cparams(("parallel",)),
        name="rope_prep",
    )(z, z, z, cos_t, sin_t)


ATT_KV_BLOCKS = 2


def _attn_body(q1_ref, q2_ref, k_ref, v_ref, lam_ref, sub_ref, o_ref,
               m1_ref, a1_ref, m2_ref, a2_ref, *, tq):
    i = pl.program_id(2)

    m1_ref[...] = jnp.full_like(m1_ref, NEG_BIG)
    m2_ref[...] = jnp.full_like(m2_ref, NEG_BIG)
    a1_ref[...] = jnp.zeros_like(a1_ref)
    a2_ref[...] = jnp.zeros_like(a2_ref)
    def update(j, nblk, masked):
        tk = nblk * tq
        rows = pl.ds(pl.multiple_of(j * tq, tq), tk)
        kb = k_ref[0, rows, :]
        vext = jnp.concatenate([v_ref[0, rows, :], jnp.ones((tk, B_V), BF16)], axis=1)
        if masked:
            row = lax.broadcasted_iota(jnp.int32, (tq, tk), 0)
            col = lax.broadcasted_iota(jnp.int32, (tq, tk), 1)
            visible = (col // CHUNK) <= (row // CHUNK)
        comps = ((q1_ref, m1_ref, a1_ref), (q2_ref, m2_ref, a2_ref))
        scores = [lax.dot_general(q_ref[0], kb, (((1,), (1,)), ((), ())), preferred_element_type=F32)
                  for q_ref, _, _ in comps]
        probs, alphas = [], []
        for s, (_, m_ref, _) in zip(scores, comps):
            if masked:
                s = jnp.where(visible, s, NEG_BIG)
            cols = [s[:, c * 128:(c + 1) * 128] for c in range(tk // 128)]
            mx = cols[0]
            for c in cols[1:]:
                mx = jnp.maximum(mx, c)
            m_old = m_ref[...]
            m_new = jnp.maximum(m_old, jnp.max(mx, axis=-1, keepdims=True))
            m_ref[...] = m_new
            alphas.append(jnp.exp(m_old - m_new))
            probs.append(jnp.concatenate([jnp.exp((c - m_new).astype(BF16)) for c in cols], axis=1))
        pvs = [jnp.dot(p, vext, preferred_element_type=F32) for p in probs]
        for pv, alpha, (_, _, a_ref) in zip(pvs, alphas, comps):
            a_ref[...] = jnp.concatenate([alpha, alpha], axis=1) * a_ref[...] + pv

    def body(jj, carry):
        update(ATT_KV_BLOCKS * jj, ATT_KV_BLOCKS, False)
        return carry

    lax.fori_loop(0, i // ATT_KV_BLOCKS, body, 0)

    def rest(r, carry):
        update((i // ATT_KV_BLOCKS) * ATT_KV_BLOCKS + r, 1, False)
        return carry

    lax.fori_loop(0, i % ATT_KV_BLOCKS, rest, 0)
    update(i, 1, True)

    a1 = a1_ref[...]
    a2 = a2_ref[...]
    o = a1[:, :B_V] / a1[:, B_V:] - lam_ref[...] * (a2[:, :B_V] / a2[:, B_V:])
    o = o * lax.rsqrt(jnp.mean(o * o, axis=-1, keepdims=True) + EPS) * sub_ref[...]
    o_ref[0] = o.astype(o_ref.dtype)


def _diff_attention(q1, q2, k, v, lam, sub, tq):
    bsz, seq, _ = q1.shape
    nq = seq // tq
    qspec = pl.BlockSpec((1, tq, B_V), lambda b, h, i: (b, i, h))
    kspec = pl.BlockSpec((1, seq, B_V), lambda b, h, i: (b, 0, h))
    pspec = pl.BlockSpec((1, B_V), lambda b, h, i: (0, 0))
    return pl.pallas_call(
        functools.partial(_attn_body, tq=tq),
        out_shape=jax.ShapeDtypeStruct((bsz, seq, B_DIM), BF16),
        grid=(bsz, B_HEADS, nq),
        in_specs=[qspec, qspec, kspec, kspec, pspec, pspec],
        out_specs=pl.BlockSpec((1, tq, B_V), lambda b, h, i: (b, i, h)),
        scratch_shapes=[pltpu.VMEM((tq, 128), F32), pltpu.VMEM((tq, 2 * B_V), F32),
                        pltpu.VMEM((tq, 128), F32), pltpu.VMEM((tq, 2 * B_V), F32)],
        compiler_params=_cparams(("parallel", "parallel", "arbitrary")),
        name="diff_attention",
    )(q1, q2, k, v, lam, sub)


S5_SUB = 8
S5_STEPS = (1, 2, 4)


def _s5_body(u_ref, bd_ref, cd_ref, pw_ref, lv_ref, d_ref, gw_ref, gb_ref, o_ref, xs_ref, carry_ref, *, lt):
    t = pl.program_id(1)

    @pl.when(t == 0)
    def _():
        carry_ref[...] = jnp.zeros_like(carry_ref)

    u_b = u_ref[...]
    u = u_b.astype(F32)
    bu = jnp.dot(u_b, bd_ref[...], preferred_element_type=F32)
    re = bu[:, :C_NS]
    im = bu[:, C_NS:]

    groups = lt // S5_SUB
    re = re.reshape(groups, S5_SUB, C_NS)
    im = im.reshape(groups, S5_SUB, C_NS)
    for level, d in enumerate(S5_STEPS):
        pr = lv_ref[level, :, :C_NS][None]
        pi = lv_ref[level, :, C_NS:][None]
        sre = pltpu.roll(re, d, 1)
        sim = pltpu.roll(im, d, 1)
        re, im = re + pr * sre - pi * sim, im + pr * sim + pi * sre
    xs_ref[:, :C_NS] = re.reshape(lt, C_NS)
    xs_ref[:, C_NS:] = im.reshape(lt, C_NS)

    pw_re = pw_ref[:, :C_NS]
    pw_im = pw_ref[:, C_NS:]

    def group(gidx, carry):
        cr, ci = carry
        rows = pl.ds(pl.multiple_of(gidx * S5_SUB, S5_SUB), S5_SUB)
        nre = xs_ref[rows, :C_NS] + pw_re * cr - pw_im * ci
        nim = xs_ref[rows, C_NS:] + pw_re * ci + pw_im * cr
        xs_ref[rows, :C_NS] = nre
        xs_ref[rows, C_NS:] = nim
        return nre[S5_SUB - 1:S5_SUB, :], nim[S5_SUB - 1:S5_SUB, :]

    cr, ci = lax.fori_loop(0, lt // S5_SUB, group, (carry_ref[0:1, :C_NS], carry_ref[0:1, C_NS:]))
    carry_ref[0:1, :C_NS] = cr
    carry_ref[0:1, C_NS:] = ci

    y = _dot(xs_ref[...], cd_ref[...]) + d_ref[...] * u
    y = _gelu_tanh(y)
    y = y * jax.nn.sigmoid(_dot(y, gw_ref[...]) + gb_ref[...])
    o_ref[...] = y.astype(o_ref.dtype)


def _s5_mixer(z, p, bsz, seq, lt):
    nt = seq // lt
    ublk = Z_U // C_DIM
    full = lambda shape: pl.BlockSpec(shape, lambda b, t: (0, 0))
    return pl.pallas_call(
        functools.partial(_s5_body, lt=lt),
        out_shape=jax.ShapeDtypeStruct((bsz * seq, C_DIM), BF16),
        grid=(bsz, nt),
        in_specs=[pl.BlockSpec((lt, C_DIM), lambda b, t: (b * nt + t, ublk)),
                  full((C_DIM, 2 * C_NS)), full((2 * C_NS, C_DIM)), full((S5_SUB, 2 * C_NS)),
                  pl.BlockSpec((len(S5_STEPS), S5_SUB, 2 * C_NS), lambda b, t: (0, 0, 0)),
                  full((1, C_DIM)), full((C_DIM, C_DIM)), full((1, C_DIM))],
        out_specs=pl.BlockSpec((lt, C_DIM), lambda b, t: (b * nt + t, 0)),
        scratch_shapes=[pltpu.VMEM((lt, 2 * C_NS), F32), pltpu.VMEM((S5_SUB, 2 * C_NS), F32)],
        compiler_params=_cparams(("parallel", "arbitrary")),
        name="s5_mixer",
    )(z, p["bd"], p["cd"], p["pw"], p["lv"], p["d"], p["glu_w"], p["glu_b"])


def _merge_out_body(ya_ref, yb_ref, yc_ref, g0_ref, g1_ref, g2_ref, bg_ref, pa_ref, pb_ref, pc_ref,
                    wo_ref, x_ref, o_ref):
    bg = bg_ref[...]
    m = jax.nn.sigmoid(g0_ref[...].astype(F32) + bg[:, 0:D_MODEL]) * jnp.dot(
        ya_ref[...], pa_ref[...], preferred_element_type=F32)
    m = m + jax.nn.sigmoid(g1_ref[...].astype(F32) + bg[:, D_MODEL:2 * D_MODEL]) * jnp.dot(
        yb_ref[...], pb_ref[...], preferred_element_type=F32)
    m = m + jax.nn.sigmoid(g2_ref[...].astype(F32) + bg[:, 2 * D_MODEL:]) * jnp.dot(
        yc_ref[...], pc_ref[...], preferred_element_type=F32)
    o_ref[...] = x_ref[...] + jnp.dot(m.astype(BF16), wo_ref[...], preferred_element_type=F32)


def _merge_out(ya, yb, yc, z, bg, pa, pb, pc, wo, x, tm):
    t = ya.shape[0]
    full = lambda shape: pl.BlockSpec(shape, lambda i: (0, 0))
    return pl.pallas_call(
        _merge_out_body,
        out_shape=jax.ShapeDtypeStruct((t, D_MODEL), F32),
        grid=(t // tm,),
        in_specs=[pl.BlockSpec((tm, A_DIM), lambda i: (i, 0)),
                  pl.BlockSpec((tm, B_DIM), lambda i: (i, 0)),
                  pl.BlockSpec((tm, C_DIM), lambda i: (i, 0)),
                  pl.BlockSpec((tm, D_MODEL), lambda i: (i, 0)),
                  pl.BlockSpec((tm, D_MODEL), lambda i: (i, 1)),
                  pl.BlockSpec((tm, D_MODEL), lambda i: (i, 2)),
                  full((1, N_BRANCH * D_MODEL)),
                  full((A_DIM, D_MODEL)), full((B_DIM, D_MODEL)), full((C_DIM, D_MODEL)),
                  full((D_MODEL, D_MODEL)),
                  pl.BlockSpec((tm, D_MODEL), lambda i: (i, 0))],
        out_specs=pl.BlockSpec((tm, D_MODEL), lambda i: (i, 0)),
        compiler_params=_cparams(("parallel",)),
        name="merge_out_proj",
    )(ya, yb, yc, z, z, z, bg, pa, pb, pc, wo, x)


FFN_HALO = 16


def _ffn_body(x_ref, xh_ref, g_ref, wv_ref, wg_ref, cw_ref, wd_ref, o_ref, h_ref, act_ref,
              *, tm, blocks_per_seq):
    i = pl.program_id(0)
    j = pl.program_id(1)
    last = pl.num_programs(1) - 1

    def norm(x):
        ms = jnp.mean(x * x, axis=-1, keepdims=True)
        return x * lax.rsqrt(ms + EPS) * g_ref[...]

    def up_gate():
        val = jnp.dot(h_ref[FFN_HALO:, :], wv_ref[...], preferred_element_type=F32)
        gate = jnp.dot(h_ref[...], wg_ref[...], preferred_element_type=F32)
        return val, gate

    def activation(val, gate_ext):
        gate = gate_ext[FFN_HALO:, :]
        row = lax.broadcasted_iota(jnp.int32, gate.shape, 0)
        prev1 = gate_ext[FFN_HALO - 1:FFN_HALO, :]
        prev2 = gate_ext[FFN_HALO - 2:FFN_HALO - 1, :]
        gm1 = jnp.where(row == 0, prev1, pltpu.roll(gate, 1, 0))
        gm2 = jnp.where(row == 0, prev2, jnp.where(row == 1, prev1, pltpu.roll(gate, 2, 0)))
        cw = cw_ref[...]
        conv = cw[0:1, :] * gm2 + cw[1:2, :] * gm1 + cw[2:3, :] * gate
        return (_gelu_tanh(conv) * val).astype(BF16)

    def down(slot):
        return jnp.dot(act_ref[slot], wd_ref[...], preferred_element_type=F32)

    @pl.when(j == 0)
    def _():
        h_ref[FFN_HALO:, :] = norm(x_ref[...]).astype(BF16)
        seq_start = (i % blocks_per_seq) == 0
        h_ref[0:FFN_HALO, :] = jnp.where(seq_start, 0.0, norm(xh_ref[...])).astype(BF16)
        o_ref[...] = x_ref[...]
        act_ref[0] = activation(*up_gate())

    @pl.when((j > 0) & (j < last))
    def _():
        up = up_gate()
        o_ref[...] += down((j - 1) % 2)
        act_ref[j % 2] = activation(*up)

    @pl.when(j == last)
    def _():
        o_ref[...] += down((j - 1) % 2)


def _ffn(x, g, w_up, conv_w, w_down, seq, tm, tn):
    t, d = x.shape
    nff = D_FF // tn
    halo_per_block = tm // FFN_HALO
    up_blk = lambda j: jnp.minimum(j, nff - 1)
    return pl.pallas_call(
        functools.partial(_ffn_body, tm=tm, blocks_per_seq=seq // tm),
        out_shape=jax.ShapeDtypeStruct((t, d), F32),
        grid=(t // tm, nff + 1),
        in_specs=[pl.BlockSpec((tm, d), lambda i, j: (i, 0)),
                  pl.BlockSpec((FFN_HALO, d), lambda i, j: (jnp.maximum(i * halo_per_block - 1, 0), 0)),
                  pl.BlockSpec((1, d), lambda i, j: (0, 0)),
                  pl.BlockSpec((d, tn), lambda i, j: (0, up_blk(j))),
                  pl.BlockSpec((d, tn), lambda i, j: (0, nff + up_blk(j))),
                  pl.BlockSpec((3, tn), lambda i, j: (0, up_blk(j))),
                  pl.BlockSpec((tn, d), lambda i, j: (jnp.maximum(j - 1, 0), 0))],
        out_specs=pl.BlockSpec((tm, d), lambda i, j: (i, 0)),
        scratch_shapes=[pltpu.VMEM((FFN_HALO + tm, d), BF16), pltpu.VMEM((2, tm, tn), BF16)],
        compiler_params=_cparams(("parallel", "arbitrary")),
        name="conv_glu_ffn",
    )(x, x, g, w_up, w_up, conv_w, w_down)


RW_IN = 3 * A_DIM + LORA_W + LORA_A + LORA_G
B_IN = 3 * B_DIM


def _pad_cols(w, width):
    return jnp.pad(w, ((0, 0), (0, width - w.shape[1])))


def _layout_w_in(w):
    o = 0
    rkv = w[:, o:o + 3 * A_DIM]; o += 3 * A_DIM
    zw = w[:, o:o + LORA_W]; o += LORA_W
    za = w[:, o:o + LORA_A]; o += LORA_A
    zg = w[:, o:o + LORA_G]; o += LORA_G
    qkv = w[:, o:o + B_IN]; o += B_IN
    u = w[:, o:o + C_DIM]; o += C_DIM
    gates = w[:, o:]
    out = jnp.concatenate([gates, rkv, qkv, _pad_cols(zw, LORA_PAD), _pad_cols(za, LORA_PAD), zg, u], axis=1)
    return out.astype(BF16)


def _rwkv_params(mix, w0, w2, a0, a2, g2, k_k, k_a, r_k, ln_w, ln_b):
    row = lambda v: v.reshape(1, -1)
    o = 3 * A_DIM
    mix_l = jnp.concatenate([
        jnp.pad(mix[o:o + LORA_W], (0, LORA_PAD - LORA_W)),
        jnp.pad(mix[o + LORA_W:o + LORA_W + LORA_A], (0, LORA_PAD - LORA_A)),
        mix[o + LORA_W + LORA_A:]])
    pad_rows = lambda m: jnp.pad(m, ((0, LORA_PAD - m.shape[0]), (0, 0)))
    w2 = pad_rows(w2)
    w2h = w2.astype(BF16)
    w2l = (w2 - w2h.astype(F32)).astype(BF16)
    return dict(mix_r=row(mix[0:A_DIM]), mix_k=row(mix[A_DIM:2 * A_DIM]), mix_v=row(mix[2 * A_DIM:3 * A_DIM]),
                mix_l=row(mix_l), w0=row(w0), w2h=w2h, w2l=w2l, a0=row(a0), a2=pad_rows(a2).astype(BF16),
                g2=g2.astype(BF16), k_k=row(k_k), k_a=row(k_a), r_k=row(r_k), ln_w=row(ln_w), ln_b=row(ln_b))


def _s5_params(lam_re, lam_im, log_dt, b_re, b_im, c_re, c_im, d, glu_w, glu_b):
    dt = jnp.exp(log_dt)[:, None]
    er = jnp.exp(lam_re * dt)
    ab_re = er * jnp.cos(lam_im * dt)
    ab_im = er * jnp.sin(lam_im * dt)
    den = lam_re * lam_re + lam_im * lam_im
    nr = ab_re - 1.0
    f_re = (nr * lam_re + ab_im * lam_im) / den
    f_im = (ab_im * lam_re - nr * lam_im) / den
    bb_re = f_re[..., None] * b_re - f_im[..., None] * b_im
    bb_im = f_re[..., None] * b_im + f_im[..., None] * b_re
    eye_g = jnp.eye(C_GROUPS, dtype=F32)
    bd_re = jnp.einsum("gpc,gh->gchp", bb_re, eye_g).reshape(C_DIM, C_NS)
    bd_im = jnp.einsum("gpc,gh->gchp", bb_im, eye_g).reshape(C_DIM, C_NS)
    cd_re = jnp.einsum("gcp,gh->gphc", c_re, eye_g).reshape(C_NS, C_DIM)
    cd_im = jnp.einsum("gcp,gh->gphc", c_im, eye_g).reshape(C_NS, C_DIM)
    pr, pi = [ab_re.reshape(-1)], [ab_im.reshape(-1)]
    for _ in range(S5_SUB - 1):
        pr.append(pr[-1] * pr[0] - pi[-1] * pi[0])
        pi.append(pr[-2] * pi[0] + pi[-1] * pr[0])
    pw = jnp.concatenate([jnp.stack(pr), jnp.stack(pi)], axis=1)
    sub = jnp.arange(S5_SUB)[:, None]
    lv = jnp.stack([jnp.where(sub >= step, pw[step - 1][None, :], 0.0) for step in S5_STEPS])
    return dict(bd=jnp.concatenate([bd_re, bd_im], axis=1).astype(BF16),
                cd=jnp.concatenate([cd_re, -cd_im], axis=0).astype(BF16),
                pw=pw, lv=lv, d=d.reshape(1, -1), glu_w=glu_w.astype(BF16), glu_b=glu_b.reshape(1, -1))


def kernel(x, positions, norm_mix, norm_ffn, w_in, b_gate, rw_mix, rw_w0, rw_w2, rw_a0, rw_a2, rw_g2, rw_k_k, rw_k_a, rw_r_k, rw_ln_w, rw_ln_b, da_lq1, da_lk1, da_lq2, da_lk2, da_subln, s5_lam_re, s5_lam_im, s5_log_dt, s5_b_re, s5_b_im, s5_c_re, s5_c_im, s5_d, s5_glu_w, s5_glu_b, proj_a, proj_b, proj_c, w_out, ffn_up, ffn_conv, ffn_down, norm_final):
    bsz, seq, d = x.shape
    t = bsz * seq
    tm_proj = min(1024, seq)
    tm_ffn = min(512, seq)
    tm_small = min(256, seq)
    tq = min(1024, seq)
    lt = min(256, seq)

    inv_freq = ROPE_THETA ** (-jnp.arange(0, B_QK, 2, dtype=F32) / B_QK)
    ang = positions.astype(F32)[..., None] * inv_freq
    cos, sin = jnp.cos(ang), jnp.sin(ang)
    cos_t = jnp.concatenate([cos, cos, cos, cos], axis=-1).reshape(t, 128)
    sin_t = jnp.concatenate([-sin, sin, -sin, sin], axis=-1).reshape(t, 128)

    xf = x.reshape(t, d)
    for l in range(DEPTH):
        z = _norm_matmul(xf, norm_mix[l].reshape(1, d), _layout_w_in(w_in[l]), tm_proj, 512)

        rp = _rwkv_params(rw_mix[l], rw_w0[l], rw_w2[l], rw_a0[l], rw_a2[l], rw_g2[l], rw_k_k[l],
                          rw_k_a[l], rw_r_k[l].reshape(-1), rw_ln_w[l], rw_ln_b[l])
        y_a = _rwkv_mixer(z, rp, bsz, seq)

        lam_init = 0.8 - 0.6 * math.exp(-0.3 * l)
        lam = jnp.exp(jnp.sum(da_lq1[l] * da_lk1[l])) - jnp.exp(jnp.sum(da_lq2[l] * da_lk2[l])) + lam_init
        q1, q2, kr, vr = _rope_prep(z, cos_t, sin_t, tm_ffn)
        shp = (bsz, seq, B_DIM)
        y_b = _diff_attention(q1.reshape(shp), q2.reshape(shp), kr.reshape(shp), vr.reshape(shp),
                              jnp.full((1, B_V), lam, F32),
                              (da_subln[l] * (1.0 - lam_init)).reshape(1, B_V), tq).reshape(t, B_DIM)

        sp = _s5_params(s5_lam_re[l], s5_lam_im[l], s5_log_dt[l], s5_b_re[l], s5_b_im[l], s5_c_re[l],
                        s5_c_im[l], s5_d[l], s5_glu_w[l], s5_glu_b[l])
        y_c = _s5_mixer(z, sp, bsz, seq, lt)

        xf = _merge_out(y_a, y_b, y_c, z, b_gate[l].reshape(1, -1), proj_a[l].astype(BF16),
                        proj_b[l].astype(BF16), proj_c[l].astype(BF16), w_out[l].astype(BF16), xf, tm_small)
        xf = _ffn(xf, norm_ffn[l].reshape(1, d), ffn_up[l].astype(BF16), ffn_conv[l],
                  ffn_down[l].astype(BF16), seq, tm_proj, 256)
    return _final_norm(xf, norm_final.reshape(1, d), tm_ffn).reshape(bsz, seq, d)
```

```python
import functools
import math

import jax
import jax.numpy as jnp
from jax import lax
from jax.experimental import pallas as pl
from jax.experimental.pallas import tpu as pltpu

F32 = jnp.float32
BF16 = jnp.bfloat16

D_MODEL = 2048
DEPTH = 4
CHUNK = 64
EPS = 1e-6
ROPE_THETA = 10000.0

A_HEAD = 64
A_DIM = 768
A_HEADS = 12
LORA_W = 96
LORA_A = 96
LORA_G = 256
LORA_PAD = 128
GN_EPS = 64e-5

B_QK = 64
B_V = 128
B_DIM = 768
B_HEADS = 6

C_GROUP = 16
C_DIM = 512
C_GROUPS = 32
C_STATE = 64
C_NS = C_GROUPS * C_STATE

N_BRANCH = 3
D_FF = 5632

Z_RKV = N_BRANCH * D_MODEL
Z_QKV = Z_RKV + 3 * A_DIM
Z_LORA = Z_QKV + 3 * B_DIM
LORA_IN = 2 * LORA_PAD + LORA_G
Z_U = Z_LORA + LORA_IN
Z_COLS = Z_U + C_DIM

VMEM_LIMIT = 56 * 1024 * 1024
NEG_BIG = -1e30


def _cparams(sem):
    return pltpu.CompilerParams(dimension_semantics=sem, vmem_limit_bytes=VMEM_LIMIT)


def _gelu_tanh(x):
    return 0.5 * x * (1.0 + jnp.tanh(math.sqrt(2.0 / math.pi) * (x + 0.044715 * (x * x * x))))


def _dot(a, b):
    return jnp.dot(a.astype(BF16), b.astype(BF16), preferred_element_type=F32)


def _dot_nt(a, b):
    return lax.dot_general(a.astype(BF16), b.astype(BF16), (((1,), (1,)), ((), ())),
                           preferred_element_type=F32)


def _dot_tn(a, b, precision=None):
    return lax.dot_general(a, b, (((0,), (0,)), ((), ())), preferred_element_type=F32,
                           precision=precision)


def _norm_matmul_body(x_ref, g_ref, w_ref, o_ref, h_ref):
    @pl.when(pl.program_id(1) == 0)
    def _():
        x = x_ref[...]
        ms = jnp.mean(x * x, axis=-1, keepdims=True)
        h_ref[...] = (x * lax.rsqrt(ms + EPS) * g_ref[...]).astype(BF16)

    o_ref[...] = jnp.dot(h_ref[...], w_ref[...], preferred_element_type=F32).astype(o_ref.dtype)


def _norm_matmul(x, g, w, tm, tn):
    t, d = x.shape
    n = w.shape[1]
    return pl.pallas_call(
        _norm_matmul_body,
        out_shape=jax.ShapeDtypeStruct((t, n), BF16),
        grid=(t // tm, n // tn),
        in_specs=[pl.BlockSpec((tm, d), lambda i, j: (i, 0)),
                  pl.BlockSpec((1, d), lambda i, j: (0, 0)),
                  pl.BlockSpec((d, tn), lambda i, j: (0, j))],
        out_specs=pl.BlockSpec((tm, tn), lambda i, j: (i, j)),
        scratch_shapes=[pltpu.VMEM((tm, d), BF16)],
        compiler_params=_cparams(("parallel", "arbitrary")),
        name="norm_in_proj",
    )(x, g, w)


def _final_norm_body(x_ref, g_ref, o_ref):
    x = x_ref[...]
    ms = jnp.mean(x * x, axis=-1, keepdims=True)
    o_ref[...] = x * lax.rsqrt(ms + EPS) * g_ref[...]


def _final_norm(x, g, tm):
    t, d = x.shape
    return pl.pallas_call(
        _final_norm_body,
        out_shape=jax.ShapeDtypeStruct((t, d), F32),
        grid=(t // tm,),
        in_specs=[pl.BlockSpec((tm, d), lambda i: (i, 0)),
                  pl.BlockSpec((1, d), lambda i: (0, 0))],
        out_specs=pl.BlockSpec((tm, d), lambda i: (i, 0)),
        compiler_params=_cparams(("parallel",)),
        name="final_norm",
    )(x, g)


RW_LANES = 2 * A_HEAD
RW_ROWS = 2 * CHUNK
RW_PAIRS = A_HEADS // 2
RW_NC = 4
RW_SC = 4
RW_HALO = 16


def _shift_rows(x, prev_row):
    row = lax.broadcasted_iota(jnp.int32, x.shape, 0)
    return jnp.where(row == 0, prev_row, pltpu.roll(x, 1, 0))


def _split_bf16(x, pieces):
    out = []
    for _ in range(pieces - 1):
        h = x.astype(BF16)
        out.append(h)
        x = x - h.astype(F32)
    out.append(x.astype(BF16))
    return out


def _pair_masks():
    lane_r = lax.broadcasted_iota(jnp.int32, (RW_LANES, RW_LANES), 0)
    lane_c = lax.broadcasted_iota(jnp.int32, (RW_LANES, RW_LANES), 1)
    same_head = (lane_r // A_HEAD) == (lane_c // A_HEAD)
    head0 = lax.broadcasted_iota(jnp.int32, (CHUNK, RW_LANES), 1) < A_HEAD
    return same_head, head0


def _stack_heads(x, head0):
    return jnp.concatenate([jnp.where(head0, x, 0.0), jnp.where(head0, 0.0, x)], axis=0)


def _unstack_heads(x):
    return x[0:CHUNK, :] + x[CHUNK:, :]


def _rwkv_prep_body(zr_ref, zk_ref, zv_ref, zl_ref, hr_ref, hk_ref, hv_ref, hl_ref,
                    mr_ref, mk_ref, mv_ref, ml_ref,
                    w0_ref, w2h_ref, w2l_ref, a0_ref, a2_ref, g2_ref, kk_ref, ka_ref, rk_ref,
                    w_ref, u0_ref, rt_ref, y0_ref, lrb_ref, kend_ref, bend_ref, v_ref, gam_ref,
                    g_ref, bonus_ref):
    first = pl.program_id(1) == 0

    def token_shift(z_ref, halo_ref, mix_ref):
        z = z_ref[...].astype(F32)
        prev = jnp.where(first, 0.0, halo_ref[RW_HALO - 1:RW_HALO, :].astype(F32))
        return z + (_shift_rows(z, prev) - z) * mix_ref[...]

    r = token_shift(zr_ref, hr_ref, mr_ref)
    k = token_shift(zk_ref, hk_ref, mk_ref)
    v = token_shift(zv_ref, hv_ref, mv_ref)
    zl = token_shift(zl_ref, hl_ref, ml_ref)
    zw = zl[:, 0:LORA_PAD]
    za = zl[:, LORA_PAD:2 * LORA_PAD]
    zg = zl[:, 2 * LORA_PAD:]

    th_h, th_l = _split_bf16(jnp.tanh(zw), 2)
    w2h = w2h_ref[...]
    pre_w = (w0_ref[...] + jnp.dot(th_h, w2h, preferred_element_type=F32)
             + jnp.dot(th_h, w2l_ref[...], preferred_element_type=F32)
             + jnp.dot(th_l, w2h, preferred_element_type=F32))
    neg = -pre_w
    softplus = jnp.maximum(neg, 0.0) + jnp.log(1.0 + jnp.exp(-jnp.abs(neg)))
    logw = -jnp.exp(-softplus - 0.5)
    a = jax.nn.sigmoid(a0_ref[...] + _dot(za, a2_ref[...]))
    g = _dot(jax.nn.sigmoid(zg), g2_ref[...])
    g_ref[...] = g.astype(g_ref.dtype)
    v_ref[...] = v.astype(v_ref.dtype)

    rows = RW_NC * CHUNK
    ti = lax.broadcasted_iota(jnp.int32, (rows, rows), 0)
    tj = lax.broadcasted_iota(jnp.int32, (rows, rows), 1)
    tri = ((tj <= ti) & ((ti // CHUNK) == (tj // CHUNK))).astype(BF16)
    lg = sum(jnp.dot(tri, piece, preferred_element_type=F32) for piece in _split_bf16(logw, 3))
    rss = [slice(ci * CHUNK, (ci + 1) * CHUNK) for ci in range(RW_NC)]
    lg_last = [lg[rs.stop - 1:rs.stop, :] for rs in rss]
    for ci in range(RW_NC):
        gam_ref[ci] = jnp.exp(lg_last[ci])

    kk_raw = k * kk_ref[...]
    k = k * (1.0 + (a - 1.0) * ka_ref[...])
    e_pos = jnp.exp(lg)
    e_neg = jnp.exp(-lg)
    e_prev = jnp.exp(lg - logw)
    e_end = jnp.concatenate([jnp.exp(last - lg[rs, :]) for last, rs in zip(lg_last, rss)], axis=0)
    rt_ref[...] = (r * e_pos).astype(rt_ref.dtype)
    kend_ref[...] = (k * e_end).astype(kend_ref.dtype)
    k_t = k * e_neg
    rkr = r * k * rk_ref[...]

    same_head, head0 = _pair_masks()
    head_ones = same_head.astype(BF16)
    sr = lax.broadcasted_iota(jnp.int32, (RW_ROWS, RW_ROWS), 0)
    sc = lax.broadcasted_iota(jnp.int32, (RW_ROWS, RW_ROWS), 1)
    same_blk = (sr // CHUNK) == (sc // CHUNK)
    strict = same_blk & ((sr % CHUNK) > (sc % CHUNK))
    incl = same_blk & ((sr % CHUNK) >= (sc % CHUNK))
    eye = (sr == sc).astype(F32)
    level_masks = []
    s = 1
    while s < CHUNK:
        level_masks.append(((sr // (2 * s)) == (sc // (2 * s))) & (((sr // s) % 2) == 1) & (((sc // s) % 2) == 0))
        s *= 2

    at = [(rs, slice(p * RW_LANES, (p + 1) * RW_LANES)) for rs in rss for p in range(RW_PAIRS)]
    chains = range(len(at))
    hsum = [jnp.dot(jnp.concatenate([kk_raw[ix] * kk_raw[ix], rkr[ix]], axis=0).astype(BF16), head_ones,
                    preferred_element_type=F32) for ix in at]
    kk = [kk_raw[ix] / jnp.maximum(jnp.sqrt(hs[0:CHUNK, :]), 1e-12) for ix, hs in zip(at, hsum)]
    kka = [kk[c] * a[at[c]] for c in chains]
    for c in chains:
        bend_ref[at[c]] = (kka[c] * e_end[at[c]]).astype(bend_ref.dtype)
        bonus_ref[at[c]] = (hsum[c][CHUNK:, :] * v[at[c]]).astype(bonus_ref.dtype)

    a_s = [_stack_heads(-kk[c] * e_prev[at[c]], head0).astype(BF16) for c in chains]
    b_s = [_stack_heads(kka[c] * e_neg[at[c]], head0).astype(BF16) for c in chains]
    k_s = [_stack_heads(k_t[ix], head0).astype(BF16) for ix in at]
    r_s = [_stack_heads(r[ix] * e_pos[ix], head0).astype(BF16) for ix in at]
    v_s = [_stack_heads(v[ix], head0).astype(BF16) for ix in at]

    ar_bk = [_dot_nt(jnp.concatenate([a_s[c], r_s[c]], axis=0), jnp.concatenate([b_s[c], k_s[c]], axis=0))
             for c in chains]
    l_ab = [jnp.where(strict, m[0:RW_ROWS, 0:RW_ROWS], 0.0) for m in ar_bk]
    l_ak = [jnp.where(strict, m[0:RW_ROWS, RW_ROWS:], 0.0) for m in ar_bk]
    l_rb = [jnp.where(incl, m[RW_ROWS:, 0:RW_ROWS], 0.0) for m in ar_bk]
    l_rk = [jnp.where(incl, m[RW_ROWS:, RW_ROWS:], 0.0) for m in ar_bk]
    for c in chains:
        lrb_ref[at[c]] = _unstack_heads(l_rb[c]).astype(lrb_ref.dtype)
    yx0 = [_dot(jnp.concatenate([l_rk[c], l_ak[c]], axis=0), v_s[c]) for c in chains]
    for c in chains:
        y0_ref[at[c]] = _unstack_heads(yx0[c][0:RW_ROWS, :])

    t_inv = [eye + jnp.where(level_masks[0], l_ab[c], 0.0) for c in chains]
    for m in level_masks[1:]:
        lt = [_dot(jnp.where(m, l_ab[c], 0.0), t_inv[c]) for c in chains]
        t_inv = [t_inv[c] + _dot(t_inv[c], lt[c]) for c in chains]

    wu = [_dot(t_inv[c], jnp.concatenate([a_s[c], yx0[c][RW_ROWS:, :].astype(BF16)], axis=1)) for c in chains]
    for c in chains:
        w_ref[at[c]] = _unstack_heads(wu[c][:, 0:RW_LANES]).astype(w_ref.dtype)
        u0_ref[at[c]] = _unstack_heads(wu[c][:, RW_LANES:])


def _rwkv_scan_body(w_ref, u0_ref, rt_ref, y0_ref, lrb_ref, kend_ref, bend_ref, v_ref, gam_ref,
                    g_ref, bonus_ref, lnw_ref, lnb_ref, o_ref, s_ref):
    @pl.when(pl.program_id(1) == 0)
    def _():
        s_ref[...] = jnp.zeros_like(s_ref)

    same_head, head0 = _pair_masks()
    head_avg = (same_head.astype(F32) * (1.0 / A_HEAD)).astype(BF16)

    pairs = range(RW_PAIRS)
    sls = [slice(p * RW_LANES, (p + 1) * RW_LANES) for p in pairs]
    s = [s_ref[p] for p in pairs]
    for ci in range(RW_SC):
        rs = slice(ci * CHUNK, (ci + 1) * CHUNK)
        gam = gam_ref[ci]
        ws = [lax.dot_general(jnp.concatenate([w_ref[rs, sl], rt_ref[rs, sl]], axis=0), s[p].astype(BF16),
                              (((1,), (1,)), ((), ())), preferred_element_type=F32)
              for p, sl in zip(pairs, sls)]
        u = [ws[p][0:CHUNK, :] + u0_ref[rs, sls[p]] for p in pairs]
        s_add = [_dot_tn(jnp.concatenate([v_ref[rs, sl], u[p].astype(BF16)], axis=0),
                         jnp.concatenate([kend_ref[rs, sl], bend_ref[rs, sl]], axis=0))
                 for p, sl in zip(pairs, sls)]
        s = [s[p] * gam[:, sls[p]] + jnp.where(same_head, s_add[p], 0.0) for p in pairs]
        y = [ws[p][CHUNK:, :] + y0_ref[rs, sls[p]]
             + jnp.dot(lrb_ref[rs, sls[p]], _stack_heads(u[p], head0).astype(BF16), preferred_element_type=F32)
             for p in pairs]

        y_hl = [_split_bf16(y[p], 2) for p in pairs]
        mu = [jnp.dot(h, head_avg, preferred_element_type=F32) + jnp.dot(l, head_avg, preferred_element_type=F32)
              for h, l in y_hl]
        d = [y[p] - mu[p] for p in pairs]
        var = [jnp.dot((d[p] * d[p]).astype(BF16), head_avg, preferred_element_type=F32) for p in pairs]
        for p, sl in zip(pairs, sls):
            yn = d[p] * lax.rsqrt(var[p] + GN_EPS) * lnw_ref[:, sl] + lnb_ref[:, sl]
            o_ref[rs, sl] = ((yn + bonus_ref[rs, sl]) * g_ref[rs, sl].astype(F32)).astype(o_ref.dtype)
    for p in pairs:
        s_ref[p] = s[p]


def _rwkv_mixer(z, p, bsz, seq):
    nchunk = seq // CHUNK
    t = bsz * seq
    rblk = Z_RKV // A_DIM
    lblk = Z_LORA // LORA_IN
    prows = RW_NC * CHUNK
    nprep = seq // prows
    halo_per_step = prows // RW_HALO

    def zspec(width, blk):
        return pl.BlockSpec((prows, width), lambda b, c: (b * nprep + c, blk))

    def hspec(width, blk):
        return pl.BlockSpec((RW_HALO, width),
                            lambda b, c: (jnp.maximum((b * nprep + c) * halo_per_step - 1, 0), blk))

    full = lambda shape: pl.BlockSpec(shape, lambda b, c: (0, 0))
    ptok = pl.BlockSpec((prows, A_DIM), lambda b, c: (b * nprep + c, 0))
    pgam = pl.BlockSpec((RW_NC, 1, A_DIM), lambda b, c: (b * nprep + c, 0, 0))
    nscan = nchunk // RW_SC
    tok = pl.BlockSpec((RW_SC * CHUNK, A_DIM), lambda b, c: (b * nscan + c, 0))
    gam_spec = pl.BlockSpec((RW_SC, 1, A_DIM), lambda b, c: (b * nscan + c, 0, 0))
    bf = jax.ShapeDtypeStruct((t, A_DIM), BF16)
    f32 = jax.ShapeDtypeStruct((t, A_DIM), F32)
    gam_shape = jax.ShapeDtypeStruct((bsz * nchunk, 1, A_DIM), F32)

    prep = pl.pallas_call(
        _rwkv_prep_body,
        out_shape=(bf, f32, bf, f32, bf, bf, bf, bf, gam_shape, bf, f32),
        grid=(bsz, nprep),
        in_specs=[zspec(A_DIM, rblk), zspec(A_DIM, rblk + 1), zspec(A_DIM, rblk + 2), zspec(LORA_IN, lblk),
                  hspec(A_DIM, rblk), hspec(A_DIM, rblk + 1), hspec(A_DIM, rblk + 2), hspec(LORA_IN, lblk),
                  full((1, A_DIM)), full((1, A_DIM)), full((1, A_DIM)), full((1, LORA_IN)),
                  full((1, A_DIM)), full((LORA_PAD, A_DIM)), full((LORA_PAD, A_DIM)),
                  full((1, A_DIM)), full((LORA_PAD, A_DIM)), full((LORA_G, A_DIM)),
                  full((1, A_DIM)), full((1, A_DIM)), full((1, A_DIM))],
        out_specs=(ptok, ptok, ptok, ptok, ptok, ptok, ptok, ptok, pgam, ptok, ptok),
        compiler_params=_cparams(("parallel", "parallel")),
        name="rwkv7_prep",
    )(z, z, z, z, z, z, z, z,
      p["mix_r"], p["mix_k"], p["mix_v"], p["mix_l"],
      p["w0"], p["w2h"], p["w2l"], p["a0"], p["a2"], p["g2"], p["k_k"], p["k_a"], p["r_k"])

    return pl.pallas_call(
        _rwkv_scan_body,
        out_shape=bf,
        grid=(bsz, nscan),
        in_specs=[tok, tok, tok, tok, tok, tok, tok, tok, gam_spec, tok, tok,
                  full((1, A_DIM)), full((1, A_DIM))],
        out_specs=tok,
        scratch_shapes=[pltpu.VMEM((RW_PAIRS, RW_LANES, RW_LANES), F32)],
        compiler_params=_cparams(("parallel", "arbitrary")),
        name="rwkv7_scan",
    )(*prep, p["ln_w"], p["ln_b"])


def _rope_body(zq_ref, zk_ref, zv_ref, cos_ref, sin_ref, q1_ref, q2_ref, k_ref, v_ref):
    reps = B_DIM // 128
    cos = jnp.concatenate([cos_ref[...]] * reps, axis=1)
    sin = jnp.concatenate([sin_ref[...]] * reps, axis=1)
    lane = lax.broadcasted_iota(jnp.int32, cos.shape, 1)
    first_half = (lane % B_QK) < (B_QK // 2)
    comp0 = (lane % B_V) < B_QK

    def rope(x):
        partner = jnp.where(first_half, pltpu.roll(x, B_DIM - B_QK // 2, 1), pltpu.roll(x, B_QK // 2, 1))
        return x * cos + partner * sin

    q = rope(zq_ref[...].astype(F32)) * (B_QK ** -0.5)
    q1_ref[...] = jnp.where(comp0, q, 0.0).astype(BF16)
    q2_ref[...] = jnp.where(comp0, 0.0, q).astype(BF16)
    k_ref[...] = rope(zk_ref[...].astype(F32)).astype(BF16)
    v_ref[...] = zv_ref[...]


def _rope_prep(z, cos_t, sin_t, tm):
    t = z.shape[0]
    qblk = Z_QKV // B_DIM
    out = jax.ShapeDtypeStruct((t, B_DIM), BF16)
    ospec = pl.BlockSpec((tm, B_DIM), lambda i: (i, 0))
    return pl.pallas_call(
        _rope_body,
        out_shape=(out, out, out, out),
        grid=(t // tm,),
        in_specs=[pl.BlockSpec((tm, B_DIM), lambda i: (i, qblk)),
                  pl.BlockSpec((tm, B_DIM), lambda i: (i, qblk + 1)),
                  pl.BlockSpec((tm, B_DIM), lambda i: (i, qblk + 2)),
                  pl.BlockSpec((tm, 128), lambda i: (i, 0)),
                  pl.BlockSpec((tm, 128), lambda i: (i, 0))],
        out_specs=(ospec, ospec, ospec, ospec),
        compiler_params=_cparams(("parallel",)),
        name="rope_prep",
    )(z, z, z, cos_t, sin_t)


ATT_KV_BLOCKS = 2


def _attn_body(q1_ref, q2_ref, k_ref, v_ref, lam_ref, sub_ref, o_ref,
               m1_ref, a1_ref, m2_ref, a2_ref, *, tq):
    i = pl.program_id(2)

    m1_ref[...] = jnp.full_like(m1_ref, NEG_BIG)
    m2_ref[...] = jnp.full_like(m2_ref, NEG_BIG)
    a1_ref[...] = jnp.zeros_like(a1_ref)
    a2_ref[...] = jnp.zeros_like(a2_ref)
    def update(j, nblk, masked):
        tk = nblk * tq
        rows = pl.ds(pl.multiple_of(j * tq, tq), tk)
        kb = k_ref[0, rows, :]
        vext = jnp.concatenate([v_ref[0, rows, :], jnp.ones((tk, B_V), BF16)], axis=1)
        if masked:
            row = lax.broadcasted_iota(jnp.int32, (tq, tk), 0)
            col = lax.broadcasted_iota(jnp.int32, (tq, tk), 1)
            visible = (col // CHUNK) <= (row // CHUNK)
        comps = ((q1_ref, m1_ref, a1_ref), (q2_ref, m2_ref, a2_ref))
        scores = [lax.dot_general(q_ref[0], kb, (((1,), (1,)), ((), ())), preferred_element_type=F32)
                  for q_ref, _, _ in comps]
        probs, alphas = [], []
        for s, (_, m_ref, _) in zip(scores, comps):
            if masked:
                s = jnp.where(visible, s, NEG_BIG)
            cols = [s[:, c * 128:(c + 1) * 128] for c in range(tk // 128)]
            mx = cols[0]
            for c in cols[1:]:
                mx = jnp.maximum(mx, c)
            m_old = m_ref[...]
            m_new = jnp.maximum(m_old, jnp.max(mx, axis=-1, keepdims=True))
            m_ref[...] = m_new
            alphas.append(jnp.exp(m_old - m_new))
            probs.append(jnp.concatenate([jnp.exp((c - m_new).astype(BF16)) for c in cols], axis=1))
        pvs = [jnp.dot(p, vext, preferred_element_type=F32) for p in probs]
        for pv, alpha, (_, _, a_ref) in zip(pvs, alphas, comps):
            a_ref[...] = jnp.concatenate([alpha, alpha], axis=1) * a_ref[...] + pv

    def body(jj, carry):
        update(ATT_KV_BLOCKS * jj, ATT_KV_BLOCKS, False)
        return carry

    lax.fori_loop(0, i // ATT_KV_BLOCKS, body, 0)

    def rest(r, carry):
        update((i // ATT_KV_BLOCKS) * ATT_KV_BLOCKS + r, 1, False)
        return carry

    lax.fori_loop(0, i % ATT_KV_BLOCKS, rest, 0)
    update(i, 1, True)

    a1 = a1_ref[...]
    a2 = a2_ref[...]
    o = a1[:, :B_V] / a1[:, B_V:] - lam_ref[...] * (a2[:, :B_V] / a2[:, B_V:])
    o = o * lax.rsqrt(jnp.mean(o * o, axis=-1, keepdims=True) + EPS) * sub_ref[...]
    o_ref[0] = o.astype(o_ref.dtype)


def _diff_attention(q1, q2, k, v, lam, sub, tq):
    bsz, seq, _ = q1.shape
    nq = seq // tq
    qspec = pl.BlockSpec((1, tq, B_V), lambda b, h, i: (b, i, h))
    kspec = pl.BlockSpec((1, seq, B_V), lambda b, h, i: (b, 0, h))
    pspec = pl.BlockSpec((1, B_V), lambda b, h, i: (0, 0))
    return pl.pallas_call(
        functools.partial(_attn_body, tq=tq),
        out_shape=jax.ShapeDtypeStruct((bsz, seq, B_DIM), BF16),
        grid=(bsz, B_HEADS, nq),
        in_specs=[qspec, qspec, kspec, kspec, pspec, pspec],
        out_specs=pl.BlockSpec((1, tq, B_V), lambda b, h, i: (b, i, h)),
        scratch_shapes=[pltpu.VMEM((tq, 128), F32), pltpu.VMEM((tq, 2 * B_V), F32),
                        pltpu.VMEM((tq, 128), F32), pltpu.VMEM((tq, 2 * B_V), F32)],
        compiler_params=_cparams(("parallel", "parallel", "arbitrary")),
        name="diff_attention",
    )(q1, q2, k, v, lam, sub)


S5_SUB = 8
S5_STEPS = (1, 2, 4)


def _s5_body(u_ref, bd_ref, cd_ref, pw_ref, lv_ref, d_ref, gw_ref, gb_ref, o_ref, xs_ref, carry_ref, *, lt):
    t = pl.program_id(1)

    @pl.when(t == 0)
    def _():
        carry_ref[...] = jnp.zeros_like(carry_ref)

    u_b = u_ref[...]
    u = u_b.astype(F32)
    bu = jnp.dot(u_b, bd_ref[...], preferred_element_type=F32)
    re = bu[:, :C_NS]
    im = bu[:, C_NS:]

    groups = lt // S5_SUB
    re = re.reshape(groups, S5_SUB, C_NS)
    im = im.reshape(groups, S5_SUB, C_NS)
    for level, d in enumerate(S5_STEPS):
        pr = lv_ref[level, :, :C_NS][None]
        pi = lv_ref[level, :, C_NS:][None]
        sre = pltpu.roll(re, d, 1)
        sim = pltpu.roll(im, d, 1)
        re, im = re + pr * sre - pi * sim, im + pr * sim + pi * sre
    xs_ref[:, :C_NS] = re.reshape(lt, C_NS)
    xs_ref[:, C_NS:] = im.reshape(lt, C_NS)

    pw_re = pw_ref[:, :C_NS]
    pw_im = pw_ref[:, C_NS:]

    def group(gidx, carry):
        cr, ci = carry
        rows = pl.ds(pl.multiple_of(gidx * S5_SUB, S5_SUB), S5_SUB)
        nre = xs_ref[rows, :C_NS] + pw_re * cr - pw_im * ci
        nim = xs_ref[rows, C_NS:] + pw_re * ci + pw_im * cr
        xs_ref[rows, :C_NS] = nre
        xs_ref[rows, C_NS:] = nim
        return nre[S5_SUB - 1:S5_SUB, :], nim[S5_SUB - 1:S5_SUB, :]

    cr, ci = lax.fori_loop(0, lt // S5_SUB, group, (carry_ref[0:1, :C_NS], carry_ref[0:1, C_NS:]))
    carry_ref[0:1, :C_NS] = cr
    carry_ref[0:1, C_NS:] = ci

    y = _dot(xs_ref[...], cd_ref[...]) + d_ref[...] * u
    y = _gelu_tanh(y)
    y = y * jax.nn.sigmoid(_dot(y, gw_ref[...]) + gb_ref[...])
    o_ref[...] = y.astype(o_ref.dtype)


def _s5_mixer(z, p, bsz, seq, lt):
    nt = seq // lt
    ublk = Z_U // C_DIM
    full = lambda shape: pl.BlockSpec(shape, lambda b, t: (0, 0))
    return pl.pallas_call(
        functools.partial(_s5_body, lt=lt),
        out_shape=jax.ShapeDtypeStruct((bsz * seq, C_DIM), BF16),
        grid=(bsz, nt),
        in_specs=[pl.BlockSpec((lt, C_DIM), lambda b, t: (b * nt + t, ublk)),
                  full((C_DIM, 2 * C_NS)), full((2 * C_NS, C_DIM)), full((S5_SUB, 2 * C_NS)),
                  pl.BlockSpec((len(S5_STEPS), S5_SUB, 2 * C_NS), lambda b, t: (0, 0, 0)),
                  full((1, C_DIM)), full((C_DIM, C_DIM)), full((1, C_DIM))],
        out_specs=pl.BlockSpec((lt, C_DIM), lambda b, t: (b * nt + t, 0)),
        scratch_shapes=[pltpu.VMEM((lt, 2 * C_NS), F32), pltpu.VMEM((S5_SUB, 2 * C_NS), F32)],
        compiler_params=_cparams(("parallel", "arbitrary")),
        name="s5_mixer",
    )(z, p["bd"], p["cd"], p["pw"], p["lv"], p["d"], p["glu_w"], p["glu_b"])


def _merge_out_body(ya_ref, yb_ref, yc_ref, g0_ref, g1_ref, g2_ref, bg_ref, pa_ref, pb_ref, pc_ref,
                    wo_ref, x_ref, o_ref):
    bg = bg_ref[...]
    m = jax.nn.sigmoid(g0_ref[...].astype(F32) + bg[:, 0:D_MODEL]) * jnp.dot(
        ya_ref[...], pa_ref[...], preferred_element_type=F32)
    m = m + jax.nn.sigmoid(g1_ref[...].astype(F32) + bg[:, D_MODEL:2 * D_MODEL]) * jnp.dot(
        yb_ref[...], pb_ref[...], preferred_element_type=F32)
    m = m + jax.nn.sigmoid(g2_ref[...].astype(F32) + bg[:, 2 * D_MODEL:]) * jnp.dot(
        yc_ref[...], pc_ref[...], preferred_element_type=F32)
    o_ref[...] = x_ref[...] + jnp.dot(m.astype(BF16), wo_ref[...], preferred_element_type=F32)


def _merge_out(ya, yb, yc, z, bg, pa, pb, pc, wo, x, tm):
    t = ya.shape[0]
    full = lambda shape: pl.BlockSpec(shape, lambda i: (0, 0))
    return pl.pallas_call(
        _merge_out_body,
        out_shape=jax.ShapeDtypeStruct((t, D_MODEL), F32),
        grid=(t // tm,),
        in_specs=[pl.BlockSpec((tm, A_DIM), lambda i: (i, 0)),
                  pl.BlockSpec((tm, B_DIM), lambda i: (i, 0)),
                  pl.BlockSpec((tm, C_DIM), lambda i: (i, 0)),
                  pl.BlockSpec((tm, D_MODEL), lambda i: (i, 0)),
                  pl.BlockSpec((tm, D_MODEL), lambda i: (i, 1)),
                  pl.BlockSpec((tm, D_MODEL), lambda i: (i, 2)),
                  full((1, N_BRANCH * D_MODEL)),
                  full((A_DIM, D_MODEL)), full((B_DIM, D_MODEL)), full((C_DIM, D_MODEL)),
                  full((D_MODEL, D_MODEL)),
                  pl.BlockSpec((tm, D_MODEL), lambda i: (i, 0))],
        out_specs=pl.BlockSpec((tm, D_MODEL), lambda i: (i, 0)),
        compiler_params=_cparams(("parallel",)),
        name="merge_out_proj",
    )(ya, yb, yc, z, z, z, bg, pa, pb, pc, wo, x)


FFN_HALO = 16


def _ffn_body(x_ref, xh_ref, g_ref, wv_ref, wg_ref, cw_ref, wd_ref, o_ref, h_ref, act_ref,
              *, tm, blocks_per_seq):
    i = pl.program_id(0)
    j = pl.program_id(1)
    last = pl.num_programs(1) - 1

    def norm(x):
        ms = jnp.mean(x * x, axis=-1, keepdims=True)
        return x * lax.rsqrt(ms + EPS) * g_ref[...]

    def up_gate():
        val = jnp.dot(h_ref[FFN_HALO:, :], wv_ref[...], preferred_element_type=F32)
        gate = jnp.dot(h_ref[...], wg_ref[...], preferred_element_type=F32)
        return val, gate

    def activation(val, gate_ext):
        gate = gate_ext[FFN_HALO:, :]
        row = lax.broadcasted_iota(jnp.int32, gate.shape, 0)
        prev1 = gate_ext[FFN_HALO - 1:FFN_HALO, :]
        prev2 = gate_ext[FFN_HALO - 2:FFN_HALO - 1, :]
        gm1 = jnp.where(row == 0, prev1, pltpu.roll(gate, 1, 0))
        gm2 = jnp.where(row == 0, prev2, jnp.where(row == 1, prev1, pltpu.roll(gate, 2, 0)))
        cw = cw_ref[...]
        conv = cw[0:1, :] * gm2 + cw[1:2, :] * gm1 + cw[2:3, :] * gate
        return (_gelu_tanh(conv) * val).astype(BF16)

    def down(slot):
        return jnp.dot(act_ref[slot], wd_ref[...], preferred_element_type=F32)

    @pl.when(j == 0)
    def _():
        h_ref[FFN_HALO:, :] = norm(x_ref[...]).astype(BF16)
        seq_start = (i % blocks_per_seq) == 0
        h_ref[0:FFN_HALO, :] = jnp.where(seq_start, 0.0, norm(xh_ref[...])).astype(BF16)
        o_ref[...] = x_ref[...]
        act_ref[0] = activation(*up_gate())

    @pl.when((j > 0) & (j < last))
    def _():
        up = up_gate()
        o_ref[...] += down((j - 1) % 2)
        act_ref[j % 2] = activation(*up)

    @pl.when(j == last)
    def _():
        o_ref[...] += down((j - 1) % 2)


def _ffn(x, g, w_up, conv_w, w_down, seq, tm, tn):
    t, d = x.shape
    nff = D_FF // tn
    halo_per_block = tm // FFN_HALO
    up_blk = lambda j: jnp.minimum(j, nff - 1)
    return pl.pallas_call(
        functools.partial(_ffn_body, tm=tm, blocks_per_seq=seq // tm),
        out_shape=jax.ShapeDtypeStruct((t, d), F32),
        grid=(t // tm, nff + 1),
        in_specs=[pl.BlockSpec((tm, d), lambda i, j: (i, 0)),
                  pl.BlockSpec((FFN_HALO, d), lambda i, j: (jnp.maximum(i * halo_per_block - 1, 0), 0)),
                  pl.BlockSpec((1, d), lambda i, j: (0, 0)),
                  pl.BlockSpec((d, tn), lambda i, j: (0, up_blk(j))),
                  pl.BlockSpec((d, tn), lambda i, j: (0, nff + up_blk(j))),
                  pl.BlockSpec((3, tn), lambda i, j: (0, up_blk(j))),
                  pl.BlockSpec((tn, d), lambda i, j: (jnp.maximum(j - 1, 0), 0))],
        out_specs=pl.BlockSpec((tm, d), lambda i, j: (i, 0)),
        scratch_shapes=[pltpu.VMEM((FFN_HALO + tm, d), BF16), pltpu.VMEM((2, tm, tn), BF16)],
        compiler_params=_cparams(("parallel", "arbitrary")),
        name="conv_glu_ffn",
    )(x, x, g, w_up, w_up, conv_w, w_down)


RW_IN = 3 * A_DIM + LORA_W + LORA_A + LORA_G
B_IN = 3 * B_DIM


def _pad_cols(w, width):
    return jnp.pad(w, ((0, 0), (0, width - w.shape[1])))


def _layout_w_in(w):
    o = 0
    rkv = w[:, o:o + 3 * A_DIM]; o += 3 * A_DIM
    zw = w[:, o:o + LORA_W]; o += LORA_W
    za = w[:, o:o + LORA_A]; o += LORA_A
    zg = w[:, o:o + LORA_G]; o += LORA_G
    qkv = w[:, o:o + B_IN]; o += B_IN
    u = w[:, o:o + C_DIM]; o += C_DIM
    gates = w[:, o:]
    out = jnp.concatenate([gates, rkv, qkv, _pad_cols(zw, LORA_PAD), _pad_cols(za, LORA_PAD), zg, u], axis=1)
    return out.astype(BF16)


def _rwkv_params(mix, w0, w2, a0, a2, g2, k_k, k_a, r_k, ln_w, ln_b):
    row = lambda v: v.reshape(1, -1)
    o = 3 * A_DIM
    mix_l = jnp.concatenate([
        jnp.pad(mix[o:o + LORA_W], (0, LORA_PAD - LORA_W)),
        jnp.pad(mix[o + LORA_W:o + LORA_W + LORA_A], (0, LORA_PAD - LORA_A)),
        mix[o + LORA_W + LORA_A:]])
    pad_rows = lambda m: jnp.pad(m, ((0, LORA_PAD - m.shape[0]), (0, 0)))
    w2 = pad_rows(w2)
    w2h = w2.astype(BF16)
    w2l = (w2 - w2h.astype(F32)).astype(BF16)
    return dict(mix_r=row(mix[0:A_DIM]), mix_k=row(mix[A_DIM:2 * A_DIM]), mix_v=row(mix[2 * A_DIM:3 * A_DIM]),
                mix_l=row(mix_l), w0=row(w0), w2h=w2h, w2l=w2l, a0=row(a0), a2=pad_rows(a2).astype(BF16),
                g2=g2.astype(BF16), k_k=row(k_k), k_a=row(k_a), r_k=row(r_k), ln_w=row(ln_w), ln_b=row(ln_b))


def _s5_params(lam_re, lam_im, log_dt, b_re, b_im, c_re, c_im, d, glu_w, glu_b):
    dt = jnp.exp(log_dt)[:, None]
    er = jnp.exp(lam_re * dt)
    ab_re = er * jnp.cos(lam_im * dt)
    ab_im = er * jnp.sin(lam_im * dt)
    den = lam_re * lam_re + lam_im * lam_im
    nr = ab_re - 1.0
    f_re = (nr * lam_re + ab_im * lam_im) / den
    f_im = (ab_im * lam_re - nr * lam_im) / den
    bb_re = f_re[..., None] * b_re - f_im[..., None] * b_im
    bb_im = f_re[..., None] * b_im + f_im[..., None] * b_re
    eye_g = jnp.eye(C_GROUPS, dtype=F32)
    bd_re = jnp.einsum("gpc,gh->gchp", bb_re, eye_g).reshape(C_DIM, C_NS)
    bd_im = jnp.einsum("gpc,gh->gchp", bb_im, eye_g).reshape(C_DIM, C_NS)
    cd_re = jnp.einsum("gcp,gh->gphc", c_re, eye_g).reshape(C_NS, C_DIM)
    cd_im = jnp.einsum("gcp,gh->gphc", c_im, eye_g).reshape(C_NS, C_DIM)
    pr, pi = [ab_re.reshape(-1)], [ab_im.reshape(-1)]
    for _ in range(S5_SUB - 1):
        pr.append(pr[-1] * pr[0] - pi[-1] * pi[0])
        pi.append(pr[-2] * pi[0] + pi[-1] * pr[0])
    pw = jnp.concatenate([jnp.stack(pr), jnp.stack(pi)], axis=1)
    sub = jnp.arange(S5_SUB)[:, None]
    lv = jnp.stack([jnp.where(sub >= step, pw[step - 1][None, :], 0.0) for step in S5_STEPS])
    return dict(bd=jnp.concatenate([bd_re, bd_im], axis=1).astype(BF16),
                cd=jnp.concatenate([cd_re, -cd_im], axis=0).astype(BF16),
                pw=pw, lv=lv, d=d.reshape(1, -1), glu_w=glu_w.astype(BF16), glu_b=glu_b.reshape(1, -1))


def kernel(x, positions, norm_mix, norm_ffn, w_in, b_gate, rw_mix, rw_w0, rw_w2, rw_a0, rw_a2, rw_g2, rw_k_k, rw_k_a, rw_r_k, rw_ln_w, rw_ln_b, da_lq1, da_lk1, da_lq2, da_lk2, da_subln, s5_lam_re, s5_lam_im, s5_log_dt, s5_b_re, s5_b_im, s5_c_re, s5_c_im, s5_d, s5_glu_w, s5_glu_b, proj_a, proj_b, proj_c, w_out, ffn_up, ffn_conv, ffn_down, norm_final):
    bsz, seq, d = x.shape
    t = bsz * seq
    tm_proj = min(1024, seq)
    tm_ffn = min(512, seq)
    tm_small = min(256, seq)
    tq = min(1024, seq)
    lt = min(256, seq)

    inv_freq = ROPE_THETA ** (-jnp.arange(0, B_QK, 2, dtype=F32) / B_QK)
    ang = positions.astype(F32)[..., None] * inv_freq
    cos, sin = jnp.cos(ang), jnp.sin(ang)
    cos_t = jnp.concatenate([cos, cos, cos, cos], axis=-1).reshape(t, 128)
    sin_t = jnp.concatenate([-sin, sin, -sin, sin], axis=-1).reshape(t, 128)

    xf = x.reshape(t, d)
    for l in range(DEPTH):
        z = _norm_matmul(xf, norm_mix[l].reshape(1, d), _layout_w_in(w_in[l]), tm_proj, 512)

        rp = _rwkv_params(rw_mix[l], rw_w0[l], rw_w2[l], rw_a0[l], rw_a2[l], rw_g2[l], rw_k_k[l],
                          rw_k_a[l], rw_r_k[l].reshape(-1), rw_ln_w[l], rw_ln_b[l])
        y_a = _rwkv_mixer(z, rp, bsz, seq)

        lam_init = 0.8 - 0.6 * math.exp(-0.3 * l)
        lam = jnp.exp(jnp.sum(da_lq1[l] * da_lk1[l])) - jnp.exp(jnp.sum(da_lq2[l] * da_lk2[l])) + lam_init
        q1, q2, kr, vr = _rope_prep(z, cos_t, sin_t, tm_ffn)
        shp = (bsz, seq, B_DIM)
        y_b = _diff_attention(q1.reshape(shp), q2.reshape(shp), kr.reshape(shp), vr.reshape(shp),
                              jnp.full((1, B_V), lam, F32),
                              (da_subln[l] * (1.0 - lam_init)).reshape(1, B_V), tq).reshape(t, B_DIM)

        sp = _s5_params(s5_lam_re[l], s5_lam_im[l], s5_log_dt[l], s5_b_re[l], s5_b_im[l], s5_c_re[l],
                        s5_c_im[l], s5_d[l], s5_glu_w[l], s5_glu_b[l])
        y_c = _s5_mixer(z, sp, bsz, seq, lt)

        xf = _merge_out(y_a, y_b, y_c, z, b_gate[l].reshape(1, -1), proj_a[l].astype(BF16),
                        proj_b[l].astype(BF16), proj_c[l].astype(BF16), w_out[l].astype(BF16), xf, tm_small)
        xf = _ffn(xf, norm_ffn[l].reshape(1, d), ffn_up[l].astype(BF16), ffn_conv[l],
                  ffn_down[l].astype(BF16), seq, tm_proj, 256)
    return _final_norm(xf, norm_final.reshape(1, d), tm_ffn).reshape(bsz, seq, d)
```
